```python
import math
import jax
import jax.numpy as jnp
from jax import lax
import numpy as np

D_MODEL = 2048
BATCH = 2
SEQ = 4096
DEPTH = 4
DEC_BATCH = 8
DEC_SEQ = 8
PAST_LEN = 16384
PAGE_SIZE = 128

N_MIXERS = 3
LAYER_MIXER = tuple(i % N_MIXERS for i in range(DEPTH))
N_LRU_LAYERS = (DEPTH + N_MIXERS - 1) // N_MIXERS
ATTN_LAYER = 2

D_FF = 256 * ((8 * D_MODEL // 3 + 255) // 256)
D_RNN = 5 * D_MODEL // 4
LRU_BLOCK = 256
LRU_BLOCKS = D_RNN // LRU_BLOCK
LRU_CONV = 4
LRU_C = 8.0
CONV_WIDTH = 31
ATTN_HEADS = 8
ATTN_HEAD_DIM = 128
D_ATTN = ATTN_HEADS * 2 * ATTN_HEAD_DIM
Q_BLOCK = 128
LAMBDA_INIT = 0.8 - 0.6 * math.exp(-0.3 * ATTN_LAYER)
NEG_INF = -1e30

kernel_name = "hybrid_rglru_conformer_diffattn_step"


def rms_norm(x, g, eps=1e-6):
    xf = x.astype(jnp.float32)
    y = xf * lax.rsqrt(jnp.mean(xf * xf, axis=-1, keepdims=True) + eps)
    return (y * g.astype(jnp.float32)).astype(x.dtype)


def layer_norm(x, g, b, eps=1e-5):
    xf = x.astype(jnp.float32)
    mu = jnp.mean(xf, axis=-1, keepdims=True)
    var = jnp.mean(jnp.square(xf - mu), axis=-1, keepdims=True)
    y = (xf - mu) * lax.rsqrt(var + eps)
    return (y * g.astype(jnp.float32) + b.astype(jnp.float32)).astype(x.dtype)


def swiglu(x, wg, wu, wd):
    return (jax.nn.silu(x @ wg) * (x @ wu)) @ wd


def causal_depthwise_conv(x, buf, w, b):
    width = w.shape[0]
    xp = jnp.concatenate([buf.astype(x.dtype), x], axis=1)
    y = lax.conv_general_dilated(
        xp, w[:, None, :].astype(x.dtype), window_strides=(1,), padding="VALID",
        dimension_numbers=("NWC", "WIO", "NWC"), feature_group_count=x.shape[-1])
    return y + b, xp[:, xp.shape[1] - (width - 1):]


def rg_lru(x, h0, wa, ba, wx, bx, lam):
    B, T, R = x.shape
    xb = x.reshape(B, T, LRU_BLOCKS, LRU_BLOCK)
    r = jax.nn.sigmoid(jnp.einsum("btnj,njk->btnk", xb, wa).reshape(B, T, R) + ba).astype(jnp.float32)
    i = jax.nn.sigmoid(jnp.einsum("btnj,njk->btnk", xb, wx).reshape(B, T, R) + bx).astype(jnp.float32)
    log_a = -LRU_C * r * jax.nn.softplus(-lam.astype(jnp.float32))
    a = jnp.exp(log_a)
    u = jnp.sqrt(-jnp.expm1(2.0 * log_a)) * i * x.astype(jnp.float32)

    def step(h, au):
        a_t, u_t = au
        h = a_t * h + u_t
        return h, h

    hT, hs = lax.scan(step, h0.astype(jnp.float32), (jnp.swapaxes(a, 0, 1), jnp.swapaxes(u, 0, 1)))
    return jnp.swapaxes(hs, 0, 1).astype(x.dtype), hT.astype(h0.dtype)


def lru_block(x, h0, buf0, w_gate, w_in, conv_w, conv_b, wa, ba, wx, bx, lam, w_out):
    gate = jax.nn.gelu(x @ w_gate)
    u, new_buf = causal_depthwise_conv(x @ w_in, buf0, conv_w, conv_b)
    hs, hT = rg_lru(u, h0, wa, ba, wx, bx, lam)
    return (hs * gate) @ w_out, hT, new_buf


def conformer_conv(x, buf0, w_pw1, b_pw1, w_dw, b_dw, ln_g, ln_b, w_pw2, b_pw2):
    u = x @ w_pw1 + b_pw1
    u = u[..., :D_MODEL] * jax.nn.sigmoid(u[..., D_MODEL:])
    v, new_buf = causal_depthwise_conv(u, buf0, w_dw, b_dw)
    v = jax.nn.silu(layer_norm(v, ln_g, ln_b))
    return v @ w_pw2 + b_pw2, new_buf


def diff_qkv(x, wq, wk, wv):
    B, T, _ = x.shape
    q = (x @ wq).reshape(B, T, ATTN_HEADS, 2, ATTN_HEAD_DIM)
    k = (x @ wk).reshape(B, T, ATTN_HEADS, 2 * ATTN_HEAD_DIM)
    v = (x @ wv).reshape(B, T, ATTN_HEADS, 2 * ATTN_HEAD_DIM)
    return q, k, v


def diff_combine(q, ks, vs, masks, lam):
    scale = ATTN_HEAD_DIM ** -0.5
    scores = []
    for k, m in zip(ks, masks):
        kk = k.reshape(k.shape[0], k.shape[1], ATTN_HEADS, 2, ATTN_HEAD_DIM)
        s = jnp.einsum("bqhcd,bkhcd->bhcqk", q, kk).astype(jnp.float32) * scale
        scores.append(s if m is None else jnp.where(m, s, NEG_INF))
    p = jax.nn.softmax(jnp.concatenate(scores, axis=-1), axis=-1)
    o = 0.0
    start = 0
    for v in vs:
        n = v.shape[1]
        o = o + jnp.einsum("bhcqk,bkhe->bqhce", p[..., start:start + n].astype(v.dtype), v)
        start += n
    return o[:, :, :, 0] - lam.astype(o.dtype) * o[:, :, :, 1]


def prompt_diff_attn(q, k, v, lam):
    B, S = q.shape[0], q.shape[1]
    nblk = S // Q_BLOCK
    qb = jnp.swapaxes(q.reshape(B, nblk, Q_BLOCK, ATTN_HEADS, 2, ATTN_HEAD_DIM), 0, 1)
    kpos = jnp.arange(S)

    def one_block(args):
        qi, bi = args
        qpos = bi * Q_BLOCK + jnp.arange(Q_BLOCK)
        mask = kpos[None, :] <= qpos[:, None]
        return diff_combine(qi, [k], [v], [mask], lam)

    o = lax.map(one_block, (qb, jnp.arange(nblk)))
    return jnp.swapaxes(o, 0, 1).reshape(B, S, ATTN_HEADS, 2 * ATTN_HEAD_DIM)


def sample_diff_attn(q, k, v, past_k, past_v, lam):
    T = q.shape[1]
    causal = jnp.arange(T)[None, :] <= jnp.arange(T)[:, None]
    return diff_combine(q, [past_k, k], [past_v, v], [None, causal], lam)


def diff_out(o, subln_g, wo):
    B, T = o.shape[0], o.shape[1]
    o = rms_norm(o, subln_g, eps=1e-5) * (1.0 - LAMBDA_INIT)
    return o.reshape(B, T, D_ATTN) @ wo


def setup_inputs(seed: int = 0) -> dict:
    key = jax.random.key(seed)
    ks = iter(jax.random.split(key, 48))
    f32 = jnp.float32

    def nrm(shape, scale):
        return jax.random.normal(next(ks), shape, f32) * scale

    D, F, R = D_MODEL, D_FF, D_RNN
    n_pages = PAST_LEN // PAGE_SIZE
    n_used = DEC_BATCH * n_pages
    n_pool = n_used + n_used // 4
    kv_row = (ATTN_HEADS, 2 * ATTN_HEAD_DIM)

    x_prompt = nrm((BATCH, SEQ, D), 1.0)
    x_sample = nrm((DEC_BATCH, DEC_SEQ, D), 1.0)
    cache_k = nrm((n_pool, PAGE_SIZE) + kv_row, 1.0)
    cache_v = nrm((n_pool, PAGE_SIZE) + kv_row, 1.0)
    page_table = jax.random.permutation(next(ks), n_pool)[:n_used].reshape(DEC_BATCH, n_pages).astype(jnp.int32)
    state_lru_h = nrm((N_LRU_LAYERS, DEC_BATCH, R), 0.5)
    state_lru_conv = nrm((N_LRU_LAYERS, DEC_BATCH, LRU_CONV - 1, R), 1.0)
    state_conv = nrm((DEC_BATCH, CONV_WIDTH - 1, D), 0.5)

    norm_pre = 1.0 + nrm((DEPTH, 3, D), 0.02)
    norm_post = 1.0 + nrm((DEPTH, 3, D), 0.02)
    ffn_w_gate = nrm((DEPTH, 2, D, F), D ** -0.5)
    ffn_w_up = nrm((DEPTH, 2, D, F), D ** -0.5)
    ffn_w_down = nrm((DEPTH, 2, F, D), F ** -0.5)

    lru_w_gate = nrm((N_LRU_LAYERS, D, R), D ** -0.5)
    lru_w_in = nrm((N_LRU_LAYERS, D, R), D ** -0.5)
    lru_conv_w = nrm((N_LRU_LAYERS, LRU_CONV, R), LRU_CONV ** -0.5)
    lru_conv_b = nrm((N_LRU_LAYERS, R), 0.01)
    lru_w_a = nrm((N_LRU_LAYERS, LRU_BLOCKS, LRU_BLOCK, LRU_BLOCK), LRU_BLOCK ** -0.5)
    lru_b_a = nrm((N_LRU_LAYERS, R), 0.01)
    lru_w_x = nrm((N_LRU_LAYERS, LRU_BLOCKS, LRU_BLOCK, LRU_BLOCK), LRU_BLOCK ** -0.5)
    lru_b_x = nrm((N_LRU_LAYERS, R), 0.01)
    a_c = jax.random.uniform(next(ks), (N_LRU_LAYERS, R), f32, 0.9, 0.999)
    s = a_c ** (1.0 / LRU_C)
    lru_lambda = jnp.log(s) - jnp.log1p(-s)
    lru_w_out = nrm((N_LRU_LAYERS, R, D), R ** -0.5)

    conv_w_pw1 = nrm((D, 2 * D), D ** -0.5)
    conv_b_pw1 = nrm((2 * D,), 0.01)
    conv_w_dw = nrm((CONV_WIDTH, D), CONV_WIDTH ** -0.5)
    conv_b_dw = nrm((D,), 0.01)
    conv_ln_g = 1.0 + nrm((D,), 0.02)
    conv_ln_b = nrm((D,), 0.01)
    conv_w_pw2 = nrm((D, D), D ** -0.5)
    conv_b_pw2 = nrm((D,), 0.01)

    attn_w_q = nrm((D, D_ATTN), D ** -0.5)
    attn_w_k = nrm((D, D_ATTN), D ** -0.5)
    attn_w_v = nrm((D, D_ATTN), D ** -0.5)
    attn_w_o = nrm((D_ATTN, D), D_ATTN ** -0.5)
    attn_lambda_q1 = nrm((ATTN_HEAD_DIM,), 0.1)
    attn_lambda_k1 = nrm((ATTN_HEAD_DIM,), 0.1)
    attn_lambda_q2 = nrm((ATTN_HEAD_DIM,), 0.1)
    attn_lambda_k2 = nrm((ATTN_HEAD_DIM,), 0.1)
    attn_subln_g = 1.0 + nrm((2 * ATTN_HEAD_DIM,), 0.02)

    return {
        "x_prompt": x_prompt, "x_sample": x_sample,
        "cache_k": cache_k, "cache_v": cache_v, "page_table": page_table,
        "state_lru_h": state_lru_h, "state_lru_conv": state_lru_conv, "state_conv": state_conv,
        "norm_pre": norm_pre, "norm_post": norm_post,
        "ffn_w_gate": ffn_w_gate, "ffn_w_up": ffn_w_up, "ffn_w_down": ffn_w_down,
        "lru_w_gate": lru_w_gate, "lru_w_in": lru_w_in, "lru_conv_w": lru_conv_w, "lru_conv_b": lru_conv_b,
        "lru_w_a": lru_w_a, "lru_b_a": lru_b_a, "lru_w_x": lru_w_x, "lru_b_x": lru_b_x,
        "lru_lambda": lru_lambda, "lru_w_out": lru_w_out,
        "conv_w_pw1": conv_w_pw1, "conv_b_pw1": conv_b_pw1, "conv_w_dw": conv_w_dw, "conv_b_dw": conv_b_dw,
        "conv_ln_g": conv_ln_g, "conv_ln_b": conv_ln_b, "conv_w_pw2": conv_w_pw2, "conv_b_pw2": conv_b_pw2,
        "attn_w_q": attn_w_q, "attn_w_k": attn_w_k, "attn_w_v": attn_w_v, "attn_w_o": attn_w_o,
        "attn_lambda_q1": attn_lambda_q1, "attn_lambda_k1": attn_lambda_k1,
        "attn_lambda_q2": attn_lambda_q2, "attn_lambda_k2": attn_lambda_k2,
        "attn_subln_g": attn_subln_g,
    }


def reference(x_prompt, x_sample, cache_k, cache_v, page_table, state_lru_h, state_lru_conv, state_conv,
              norm_pre, norm_post, ffn_w_gate, ffn_w_up, ffn_w_down,
              lru_w_gate, lru_w_in, lru_conv_w, lru_conv_b, lru_w_a, lru_b_a, lru_w_x, lru_b_x,
              lru_lambda, lru_w_out,
              conv_w_pw1, conv_b_pw1, conv_w_dw, conv_b_dw, conv_ln_g, conv_ln_b, conv_w_pw2, conv_b_pw2,
              attn_w_q, attn_w_k, attn_w_v, attn_w_o,
              attn_lambda_q1, attn_lambda_k1, attn_lambda_q2, attn_lambda_k2, attn_subln_g):
    f32 = jnp.float32
    lam = (jnp.exp(jnp.sum(attn_lambda_q1.astype(f32) * attn_lambda_k1.astype(f32)))
           - jnp.exp(jnp.sum(attn_lambda_q2.astype(f32) * attn_lambda_k2.astype(f32))) + LAMBDA_INIT)

    def ffn(h, layer, j):
        return swiglu(h, ffn_w_gate[layer, j], ffn_w_up[layer, j], ffn_w_down[layer, j])

    def trunk(x, lru_h0, lru_buf0, conv_buf0, past_k, past_v):
        new_h, new_buf = [], []
        new_conv, new_k, new_v = None, None, None
        li = 0
        for layer in range(DEPTH):
            x = x + 0.5 * rms_norm(ffn(rms_norm(x, norm_pre[layer, 0]), layer, 0), norm_post[layer, 0])
            h = rms_norm(x, norm_pre[layer, 1])
            kind = LAYER_MIXER[layer]
            if kind == 0:
                out, hT, buf = lru_block(h, lru_h0[li], lru_buf0[li], lru_w_gate[li], lru_w_in[li],
                                         lru_conv_w[li], lru_conv_b[li], lru_w_a[li], lru_b_a[li],
                                         lru_w_x[li], lru_b_x[li], lru_lambda[li], lru_w_out[li])
                new_h.append(hT)
                new_buf.append(buf)
                li += 1
            elif kind == 1:
                out, new_conv = conformer_conv(h, conv_buf0, conv_w_pw1, conv_b_pw1, conv_w_dw, conv_b_dw,
                                               conv_ln_g, conv_ln_b, conv_w_pw2, conv_b_pw2)
            else:
                q, k, v = diff_qkv(h, attn_w_q, attn_w_k, attn_w_v)
                if past_k is None:
                    o = prompt_diff_attn(q, k, v, lam)
                else:
                    o = sample_diff_attn(q, k, v, past_k, past_v, lam)
                out = diff_out(o, attn_subln_g, attn_w_o)
                new_k, new_v = k, v
            x = x + rms_norm(out, norm_post[layer, 1])
            x = x + 0.5 * rms_norm(ffn(rms_norm(x, norm_pre[layer, 2]), layer, 1), norm_post[layer, 2])
        return x, new_k, new_v, jnp.stack(new_h), jnp.stack(new_buf), new_conv

    bp = x_prompt.shape[0]
    dt = x_prompt.dtype
    y_prompt, k_p, v_p, h_p, lbuf_p, cbuf_p = trunk(
        x_prompt,
        jnp.zeros((N_LRU_LAYERS, bp, D_RNN), dt),
        jnp.zeros((N_LRU_LAYERS, bp, LRU_CONV - 1, D_RNN), dt),
        jnp.zeros((bp, CONV_WIDTH - 1, D_MODEL), dt),
        None, None)

    db, n_pages = page_table.shape
    past_k = cache_k[page_table].reshape(db, n_pages * cache_k.shape[1], ATTN_HEADS, 2 * ATTN_HEAD_DIM)
    past_v = cache_v[page_table].reshape(db, n_pages * cache_v.shape[1], ATTN_HEADS, 2 * ATTN_HEAD_DIM)
    y_sample, k_s, v_s, h_s, lbuf_s, cbuf_s = trunk(
        x_sample, state_lru_h, state_lru_conv, state_conv, past_k, past_v)

    return (y_prompt, y_sample, k_p, v_p, h_p, lbuf_p, cbuf_p, k_s, v_s, h_s, lbuf_s, cbuf_s)
```

```python
import functools
import math

import jax
import jax.numpy as jnp
from jax import lax
from jax.experimental import pallas as pl
from jax.experimental.pallas import tpu as pltpu

F32 = jnp.float32
BF16 = jnp.bfloat16

LRU_BLOCK = 256
LRU_C = 8.0
LRU_CONV = 4
CONV_WIDTH = 31
ATTN_HEADS = 8
HEAD_DIM = 128
ATTN_LAYER = 2
LAMBDA_INIT = 0.8 - 0.6 * math.exp(-0.3 * ATTN_LAYER)
NEG_INF = -1e30
PAGE_SIZE = 128

V7X_VMEM_LIMIT_BYTES = 56 * 1024 * 1024
SUBLANES = 8


def _params(semantics):
    return pltpu.CompilerParams(dimension_semantics=semantics, vmem_limit_bytes=V7X_VMEM_LIMIT_BYTES)


def _rms(x, g, eps):
    return x * lax.rsqrt(jnp.mean(x * x, axis=-1, keepdims=True) + eps) * g


def _nt_dot(a, b):
    return lax.dot_general(a, b, (((1,), (1,)), ((), ())), preferred_element_type=F32)


def _lambda_full(lq1, lk1, lq2, lk2):
    s1 = jnp.sum(lq1 * lk1, axis=-1, keepdims=True)
    s2 = jnp.sum(lq2 * lk2, axis=-1, keepdims=True)
    return jnp.exp(s1) - jnp.exp(s2) + LAMBDA_INIT


def _ffn_kernel(x_ref, gpre_ref, wg_ref, wu_ref, wd_ref, gpost_ref, o_ref, h_ref, acc_ref, *, nj):
    j = pl.program_id(1)

    @pl.when(j == 0)
    def _():
        h_ref[...] = _rms(x_ref[...], gpre_ref[...], 1e-6).astype(BF16)

    h = h_ref[...]
    g = jnp.dot(h, wg_ref[...], preferred_element_type=F32)
    u = jnp.dot(h, wu_ref[...], preferred_element_type=F32)
    a = (g * jax.nn.sigmoid(g) * u).astype(BF16)
    d = jnp.dot(a, wd_ref[...], preferred_element_type=F32)

    @pl.when(j == 0)
    def _():
        acc_ref[...] = d

    @pl.when(j > 0)
    def _():
        acc_ref[...] += d

    @pl.when(j == nj - 1)
    def _():
        o_ref[...] = x_ref[...] + 0.5 * _rms(acc_ref[...], gpost_ref[...], 1e-6)


def _ffn(x, gpre, wg, wu, wd, gpost, layer, half, tm, tf):
    n, d = x.shape
    f = wg.shape[-1]
    tm = min(tm, n)
    nj = f // tf
    return pl.pallas_call(
        functools.partial(_ffn_kernel, nj=nj),
        out_shape=jax.ShapeDtypeStruct((n, d), F32),
        grid=(n // tm, nj),
        in_specs=[
            pl.BlockSpec((tm, d), lambda i, j: (i, 0)),
            pl.BlockSpec((1, d), lambda i, j: (0, 0)),
            pl.BlockSpec((None, None, d, tf), lambda i, j: (layer, half, 0, j)),
            pl.BlockSpec((None, None, d, tf), lambda i, j: (layer, half, 0, j)),
            pl.BlockSpec((None, None, tf, d), lambda i, j: (layer, half, j, 0)),
            pl.BlockSpec((1, d), lambda i, j: (0, 0)),
        ],
        out_specs=pl.BlockSpec((tm, d), lambda i, j: (i, 0)),
        scratch_shapes=[pltpu.VMEM((tm, d), BF16), pltpu.VMEM((tm, d), F32)],
        compiler_params=_params(("parallel", "arbitrary")),
        name="ffn",
    )(x, gpre, wg, wu, wd, gpost)


def _norm_proj_kernel(*refs, nw, nb, epilogue):
    x_ref, g_ref = refs[0], refs[1]
    w_refs = refs[2:2 + nw]
    b_refs = refs[2 + nw:2 + nw + nb]
    o_refs = refs[2 + nw + nb:-1]
    h_ref = refs[-1]

    @pl.when(pl.program_id(1) == 0)
    def _():
        h_ref[...] = _rms(x_ref[...], g_ref[...], 1e-6).astype(BF16)

    h = h_ref[...]
    accs = [jnp.dot(h, w[...], preferred_element_type=F32) for w in w_refs]
    if nb:
        accs = [a + b[...] for a, b in zip(accs, b_refs)]
    for o_ref, o in zip(o_refs, epilogue(accs)):
        o_ref[...] = o.astype(o_ref.dtype)


def _norm_proj(x, gpre, ws, bs, epilogue, out_dtypes, m, tm, tn, name):
    n, d = x.shape
    tm = min(tm, n)
    w_specs = [pl.BlockSpec((d, tn), functools.partial(lambda i, j, o: (0, j + o), o=off // tn)) for _, off in ws]
    b_specs = [pl.BlockSpec((1, tn), functools.partial(lambda i, j, o: (0, j + o), o=off // tn)) for _, off in bs]
    return pl.pallas_call(
        functools.partial(_norm_proj_kernel, nw=len(ws), nb=len(bs), epilogue=epilogue),
        out_shape=[jax.ShapeDtypeStruct((n, m), dt) for dt in out_dtypes],
        grid=(n // tm, m // tn),
        in_specs=[pl.BlockSpec((tm, d), lambda i, j: (i, 0)), pl.BlockSpec((1, d), lambda i, j: (0, 0))]
        + w_specs + b_specs,
        out_specs=[pl.BlockSpec((tm, tn), lambda i, j: (i, j)) for _ in out_dtypes],
        scratch_shapes=[pltpu.VMEM((tm, d), BF16)],
        compiler_params=_params(("parallel", "arbitrary")),
        name=name,
    )(x, gpre, *[w for w, _ in ws], *[b for b, _ in bs])


def _proj_res_kernel(*refs, has_bias):
    if has_bias:
        a_ref, w_ref, b_ref, x_ref, g_ref, o_ref = refs
    else:
        a_ref, w_ref, x_ref, g_ref, o_ref = refs
    y = jnp.dot(a_ref[...].astype(BF16), w_ref[...], preferred_element_type=F32)
    if has_bias:
        y = y + b_ref[...]
    o_ref[...] = x_ref[...] + _rms(y, g_ref[...], 1e-6)


def _proj_res(a, w, b, x, gpost, tm, name):
    n, k = a.shape
    d = w.shape[1]
    tm = min(tm, n)
    has_bias = b is not None
    in_specs = [pl.BlockSpec((tm, k), lambda i: (i, 0)), pl.BlockSpec((k, d), lambda i: (0, 0))]
    args = [a, w]
    if has_bias:
        in_specs.append(pl.BlockSpec((1, d), lambda i: (0, 0)))
        args.append(b)
    in_specs += [pl.BlockSpec((tm, d), lambda i: (i, 0)), pl.BlockSpec((1, d), lambda i: (0, 0))]
    args += [x, gpost]
    return pl.pallas_call(
        functools.partial(_proj_res_kernel, has_bias=has_bias),
        out_shape=jax.ShapeDtypeStruct((n, d), F32),
        grid=(n // tm,),
        in_specs=in_specs,
        out_specs=pl.BlockSpec((tm, d), lambda i: (i, 0)),
        compiler_params=_params(("parallel",)),
        name=name,
    )(*args)


def _lru_kernel(u0_ref, gate_ref, buf0_ref, h0_ref, cw_ref, cb_ref, wa_ref, ba_ref, wx_ref, bx_ref, lam_ref,
                y_ref, ht_ref, xs_ref, h_ref, *, tc, nt):
    t = pl.program_id(2)
    halo = LRU_CONV - 1

    @pl.when(t == 0)
    def _():
        xs_ref[SUBLANES - halo:SUBLANES, :] = buf0_ref[...]
        h_ref[...] = h0_ref[...]

    x = u0_ref[...]
    xs_ref[SUBLANES:SUBLANES + tc, :] = x
    cw = cw_ref[...]
    u = cw[halo:halo + 1] * x + cb_ref[...]
    for s in range(1, LRU_CONV):
        u = u + cw[halo - s:halo - s + 1] * xs_ref[SUBLANES - s:SUBLANES - s + tc, :]
    xs_ref[0:SUBLANES, :] = xs_ref[tc:tc + SUBLANES, :]

    ub = u.astype(BF16)
    r = jax.nn.sigmoid(jnp.dot(ub, wa_ref[...], preferred_element_type=F32) + ba_ref[...])
    i = jax.nn.sigmoid(jnp.dot(ub, wx_ref[...], preferred_element_type=F32) + bx_ref[...])
    lam = lam_ref[...]
    softplus_neg_lam = jnp.maximum(-lam, 0.0) + jnp.log1p(jnp.exp(-jnp.abs(lam)))
    log_a = -LRU_C * r * softplus_neg_lam
    a = jnp.exp(log_a)
    th = jnp.tanh(log_a)
    b = jnp.sqrt(-2.0 * th / (1.0 - th)) * i * u

    row = lax.broadcasted_iota(jnp.int32, (tc, LRU_BLOCK), 0)
    s = 1
    while s < tc:
        a_sh = jnp.where(row >= s, pltpu.roll(a, s, axis=0), 1.0)
        b_sh = jnp.where(row >= s, pltpu.roll(b, s, axis=0), 0.0)
        b = a * b_sh + b
        a = a * a_sh
        s *= 2
    hs = a * h_ref[...] + b
    h_ref[...] = hs[tc - 1:tc, :]
    y_ref[...] = (hs * gate_ref[...]).astype(y_ref.dtype)

    @pl.when(t == nt - 1)
    def _():
        ht_ref[...] = hs[tc - 1:tc, :]


def _lru_scan(u0, gate, buf0, h0, cw, cb, wa, ba, wx, bx, lam, tc):
    bsz, t, r = u0.shape
    tc = min(tc, t)
    nt = t // tc
    nb = r // LRU_BLOCK
    vec = pl.BlockSpec((1, LRU_BLOCK), lambda b, n, k: (0, n))
    return pl.pallas_call(
        functools.partial(_lru_kernel, tc=tc, nt=nt),
        out_shape=[jax.ShapeDtypeStruct((bsz, t, r), BF16), jax.ShapeDtypeStruct((bsz, 1, r), F32)],
        grid=(bsz, nb, nt),
        in_specs=[
            pl.BlockSpec((None, tc, LRU_BLOCK), lambda b, n, k: (b, k, n)),
            pl.BlockSpec((None, tc, LRU_BLOCK), lambda b, n, k: (b, k, n)),
            pl.BlockSpec((None, LRU_CONV - 1, LRU_BLOCK), lambda b, n, k: (b, 0, n)),
            pl.BlockSpec((None, 1, LRU_BLOCK), lambda b, n, k: (b, 0, n)),
            pl.BlockSpec((LRU_CONV, LRU_BLOCK), lambda b, n, k: (0, n)),
            vec,
            pl.BlockSpec((None, LRU_BLOCK, LRU_BLOCK), lambda b, n, k: (n, 0, 0)),
            vec,
            pl.BlockSpec((None, LRU_BLOCK, LRU_BLOCK), lambda b, n, k: (n, 0, 0)),
            vec,
            vec,
        ],
        out_specs=[
            pl.BlockSpec((None, tc, LRU_BLOCK), lambda b, n, k: (b, k, n)),
            pl.BlockSpec((None, 1, LRU_BLOCK), lambda b, n, k: (b, 0, n)),
        ],
        scratch_shapes=[pltpu.VMEM((tc + SUBLANES, LRU_BLOCK), F32), pltpu.VMEM((1, LRU_BLOCK), F32)],
        compiler_params=_params(("parallel", "parallel", "arbitrary")),
        name="lru_scan",
    )(u0, gate, buf0, h0, cw, cb, wa, ba, wx, bx, lam)


CONV_HALO = 32
CONV_LANES = 256


def _cconv_kernel(u_ref, buf0_ref, w_ref, b_ref, lng_ref, lnb_ref, o_ref, xs_ref, acc_ref, *, tc):
    t = pl.program_id(1)
    d = u_ref.shape[-1]
    pad = CONV_HALO - (CONV_WIDTH - 1)

    @pl.when(t == 0)
    def _():
        xs_ref[pad:CONV_HALO, :] = buf0_ref[...]

    xs_ref[CONV_HALO:CONV_HALO + tc, :] = u_ref[...]
    for c in range(d // CONV_LANES):
        cs = slice(c * CONV_LANES, (c + 1) * CONV_LANES)
        acc = b_ref[:, cs] + w_ref[0:1, cs] * xs_ref[pad:pad + tc, cs]
        for k in range(1, CONV_WIDTH):
            acc = acc + w_ref[k:k + 1, cs] * xs_ref[pad + k:pad + k + tc, cs]
        acc_ref[:, cs] = acc
    xs_ref[0:CONV_HALO, :] = xs_ref[tc:tc + CONV_HALO, :]

    v = acc_ref[...]
    mu = jnp.mean(v, axis=-1, keepdims=True)
    vc = v - mu
    var = jnp.mean(vc * vc, axis=-1, keepdims=True)
    y = vc * lax.rsqrt(var + 1e-5) * lng_ref[...] + lnb_ref[...]
    o_ref[...] = (y * jax.nn.sigmoid(y)).astype(o_ref.dtype)


def _cconv(u, buf0, w, b, lng, lnb, tc):
    bsz, t, d = u.shape
    tc = min(tc, t)
    vec = pl.BlockSpec((1, d), lambda bb, k: (0, 0))
    return pl.pallas_call(
        functools.partial(_cconv_kernel, tc=tc),
        out_shape=jax.ShapeDtypeStruct((bsz, t, d), BF16),
        grid=(bsz, t // tc),
        in_specs=[
            pl.BlockSpec((None, tc, d), lambda bb, k: (bb, k, 0)),
            pl.BlockSpec((None, CONV_WIDTH - 1, d), lambda bb, k: (bb, 0, 0)),
            pl.BlockSpec((CONV_WIDTH, d), lambda bb, k: (0, 0)),
            vec, vec, vec,
        ],
        out_specs=pl.BlockSpec((None, tc, d), lambda bb, k: (bb, k, 0)),
        scratch_shapes=[pltpu.VMEM((tc + CONV_HALO, d), F32), pltpu.VMEM((tc, d), F32)],
        compiler_params=_params(("parallel", "arbitrary")),
        name="conformer_conv",
    )(u, buf0, w, b, lng, lnb)


def _diff_finish(o1, o2, lam, g):
    o = o1 - lam * o2
    return _rms(o, g, 1e-5) * (1.0 - LAMBDA_INIT)


def _flash_kernel(lq1_ref, lk1_ref, lq2_ref, lk2_ref, g_ref, q_ref, k_ref, v_ref, o_ref,
                  m_ref, l_ref, acc_ref, *, tq, nk):
    qi = pl.program_id(2)
    ki = pl.program_id(3)
    scale = HEAD_DIM ** -0.5

    @pl.when(ki == 0)
    def _():
        m_ref[...] = jnp.full(m_ref.shape, NEG_INF, F32)
        l_ref[...] = jnp.zeros(l_ref.shape, F32)
        acc_ref[...] = jnp.zeros(acc_ref.shape, F32)

    def update(masked):
        q = q_ref[...]
        k = k_ref[...]
        v = v_ref[...].astype(BF16)
        if masked:
            keep = (lax.broadcasted_iota(jnp.int32, (tq, tq), 0) >= lax.broadcasted_iota(jnp.int32, (tq, tq), 1))
        for c in range(2):
            cs = slice(c * HEAD_DIM, (c + 1) * HEAD_DIM)
            s = _nt_dot(q[:, cs].astype(BF16), k[:, cs].astype(BF16)) * scale
            if masked:
                s = jnp.where(keep, s, NEG_INF)
            m_prev = m_ref[c]
            m_new = jnp.maximum(m_prev, jnp.max(s, axis=-1, keepdims=True))
            alpha = jnp.exp(m_prev - m_new)
            p = jnp.exp(s - m_new)
            l_ref[c] = alpha * l_ref[c] + jnp.sum(p, axis=-1, keepdims=True)
            acc_ref[c] = alpha * acc_ref[c] + jnp.dot(p.astype(BF16), v, preferred_element_type=F32)
            m_ref[c] = m_new

    @pl.when(ki < qi)
    def _():
        update(False)

    @pl.when(ki == qi)
    def _():
        update(True)

    @pl.when(ki == nk - 1)
    def _():
        lam = _lambda_full(lq1_ref[...], lk1_ref[...], lq2_ref[...], lk2_ref[...])
        o1 = acc_ref[0] / l_ref[0]
        o2 = acc_ref[1] / l_ref[1]
        o_ref[...] = _diff_finish(o1, o2, lam, g_ref[...]).astype(o_ref.dtype)


def _flash_diff_attn(q, k, v, lams, g, tq):
    bsz, s, _ = q.shape
    hd2 = 2 * HEAD_DIM
    nq = s // tq
    small = pl.BlockSpec((1, HEAD_DIM), lambda b, h, i, j: (0, 0))
    kv_spec = pl.BlockSpec((None, tq, hd2), lambda b, h, i, j: (b, jnp.minimum(i, j), h))
    return pl.pallas_call(
        functools.partial(_flash_kernel, tq=tq, nk=nq),
        out_shape=jax.ShapeDtypeStruct(q.shape, BF16),
        grid=(bsz, ATTN_HEADS, nq, nq),
        in_specs=[small, small, small, small,
                  pl.BlockSpec((1, hd2), lambda b, h, i, j: (0, 0)),
                  pl.BlockSpec((None, tq, hd2), lambda b, h, i, j: (b, i, h)),
                  kv_spec, kv_spec],
        out_specs=pl.BlockSpec((None, tq, hd2), lambda b, h, i, j: (b, i, h)),
        scratch_shapes=[pltpu.VMEM((2, tq, 1), F32), pltpu.VMEM((2, tq, 1), F32), pltpu.VMEM((2, tq, hd2), F32)],
        compiler_params=_params(("parallel", "parallel", "parallel", "arbitrary")),
        name="flash_diff_attn",
    )(*lams, g, q, k, v)


def _paged_kernel(pt_ref, lq1_ref, lk1_ref, lq2_ref, lk2_ref, g_ref, q_ref, kn_ref, vn_ref, *rest, pp, nj):
    del pt_ref
    k_refs = rest[:pp]
    v_refs = rest[pp:2 * pp]
    o_ref = rest[2 * pp]
    qbd_ref, m_ref, l_ref, acc_ref = rest[2 * pp + 1:]
    j = pl.program_id(1)
    t = q_ref.shape[0]
    nhc = 2 * ATTN_HEADS
    rows = nhc * t
    hd2 = 2 * HEAD_DIM
    scale = HEAD_DIM ** -0.5

    @pl.when(j == 0)
    def _():
        q = q_ref[...]
        qt = jnp.concatenate([q] * nhc, axis=0)
        r = lax.broadcasted_iota(jnp.int32, qt.shape, 0) // t
        c = lax.broadcasted_iota(jnp.int32, qt.shape, 1) // HEAD_DIM
        qbd_ref[...] = jnp.where(r == c, qt, 0.0).astype(BF16)
        m_ref[...] = jnp.full(m_ref.shape, NEG_INF, F32)
        l_ref[...] = jnp.zeros(l_ref.shape, F32)
        acc_ref[...] = jnp.zeros(acc_ref.shape, F32)

    def update(ks, vs, keep):
        qbd = qbd_ref[...]
        ss = [_nt_dot(qbd, kk.astype(BF16)) * scale for kk in ks]
        if keep is not None:
            ss = [jnp.where(keep, s, NEG_INF) for s in ss]
        m_prev = m_ref[...]
        m_new = m_prev
        for s in ss:
            m_new = jnp.maximum(m_new, jnp.max(s, axis=-1, keepdims=True))
        alpha = jnp.exp(m_prev - m_new)
        ps = [jnp.exp(s - m_new) for s in ss]
        l_new = alpha * l_ref[...]
        for p in ps:
            l_new = l_new + jnp.sum(p, axis=-1, keepdims=True)
        l_ref[...] = l_new
        m_ref[...] = m_new
        pbs = [p.astype(BF16) for p in ps]
        vbs = [vv.astype(BF16) for vv in vs]
        for h in range(ATTN_HEADS):
            rs = slice(h * 2 * t, (h + 1) * 2 * t)
            hs = slice(h * hd2, (h + 1) * hd2)
            pv = jnp.dot(pbs[0][rs], vbs[0][:, hs], preferred_element_type=F32)
            for pb, vb in zip(pbs[1:], vbs[1:]):
                pv = pv + jnp.dot(pb[rs], vb[:, hs], preferred_element_type=F32)
            acc_ref[rs, :] = alpha[rs] * acc_ref[rs, :] + pv

    update([kr[...] for kr in k_refs], [vr[...] for vr in v_refs], None)

    @pl.when(j == nj - 1)
    def _():
        qpos = lax.broadcasted_iota(jnp.int32, (rows, t), 0) % t
        kpos = lax.broadcasted_iota(jnp.int32, (rows, t), 1)
        update([kn_ref[...]], [vn_ref[...]], kpos <= qpos)
        lam = _lambda_full(lq1_ref[...], lk1_ref[...], lq2_ref[...], lk2_ref[...])
        o = acc_ref[...] / l_ref[...]
        for h in range(ATTN_HEADS):
            o1 = o[h * 2 * t:h * 2 * t + t]
            o2 = o[h * 2 * t + t:(h + 1) * 2 * t]
            o_ref[:, h * hd2:(h + 1) * hd2] = _diff_finish(o1, o2, lam, g_ref[...]).astype(o_ref.dtype)


def _paged_diff_attn(q, k_new, v_new, cache_k, cache_v, page_table, lams, g, pp):
    bsz, t, dk = q.shape
    n_pages = page_table.shape[1]
    nj = n_pages // pp
    rows = 2 * ATTN_HEADS * t
    small = pl.BlockSpec((1, HEAD_DIM), lambda b, j, pt: (0, 0))
    tok = pl.BlockSpec((None, t, dk), lambda b, j, pt: (b, 0, 0))
    page_specs = [
        pl.BlockSpec((None, PAGE_SIZE, dk), functools.partial(lambda b, j, pt, r: (pt[b, j * pp + r], 0, 0), r=r))
        for r in range(pp)
    ]
    grid_spec = pltpu.PrefetchScalarGridSpec(
        num_scalar_prefetch=1,
        grid=(bsz, nj),
        in_specs=[small, small, small, small, pl.BlockSpec((1, 2 * HEAD_DIM), lambda b, j, pt: (0, 0)),
                  tok, tok, tok] + page_specs + page_specs,
        out_specs=pl.BlockSpec((None, t, dk), lambda b, j, pt: (b, 0, 0)),
        scratch_shapes=[pltpu.VMEM((rows, dk), BF16), pltpu.VMEM((rows, 1), F32), pltpu.VMEM((rows, 1), F32),
                        pltpu.VMEM((rows, 2 * HEAD_DIM), F32)],
    )
    return pl.pallas_call(
        functools.partial(_paged_kernel, pp=pp, nj=nj),
        out_shape=jax.ShapeDtypeStruct(q.shape, BF16),
        grid_spec=grid_spec,
        compiler_params=_params(("parallel", "arbitrary")),
        name="paged_diff_attn",
    )(page_table, *lams, g, q, k_new, v_new, *([cache_k] * pp), *([cache_v] * pp))


FFN_TM = 512
FFN_TF = 512
PROJ_TM = 512
PROJ_TN = 512
OUT_TM = 256
LRU_TC = 256
CCONV_TC = 128
FLASH_TQ = 512
PAGES_PER_STEP = 4


def _gelu_gate(accs):
    return [jax.nn.gelu(accs[0], approximate=True), accs[1]]


def _glu(accs):
    return [accs[0] * jax.nn.sigmoid(accs[1])]


def _identity(accs):
    return accs


def _row(v):
    return v.reshape(1, -1)


def _trunk(x, bsz, t, lru_h0, lru_buf0, conv_buf0, past, w):
    d = x.shape[-1]
    new_h, new_buf = [], []
    new_conv = new_k = new_v = None
    li = 0
    for layer in range(4):
        x = _ffn(x, _row(w["norm_pre"][layer, 0]), w["ffn_wg"], w["ffn_wu"], w["ffn_wd"],
                 _row(w["norm_post"][layer, 0]), layer, 0, FFN_TM, FFN_TF)
        gpre = _row(w["norm_pre"][layer, 1])
        gpost = _row(w["norm_post"][layer, 1])
        kind = layer % 3
        if kind == 0:
            r = w["lru_w_gate"].shape[-1]
            gate, u0 = _norm_proj(x, gpre, [(w["lru_w_gate"][li], 0), (w["lru_w_in"][li], 0)], [], _gelu_gate,
                                  [F32, F32], r, PROJ_TM, PROJ_TN, "lru_in")
            u0 = u0.reshape(bsz, t, r)
            y, ht = _lru_scan(u0, gate.reshape(bsz, t, r), lru_buf0[li], lru_h0[li].reshape(bsz, 1, r),
                              w["lru_conv_w"][li], _row(w["lru_conv_b"][li]), w["lru_w_a"][li], _row(w["lru_b_a"][li]),
                              w["lru_w_x"][li], _row(w["lru_b_x"][li]), _row(w["lru_lambda"][li]), LRU_TC)
            new_h.append(ht.reshape(bsz, r))
            new_buf.append(jnp.concatenate([lru_buf0[li], u0], axis=1)[:, t:])
            x = _proj_res(y.reshape(bsz * t, r), w["lru_w_out"][li], None, x, gpost, OUT_TM, "lru_out")
            li += 1
        elif kind == 1:
            b1 = _row(w["conv_b_pw1"])
            (u,) = _norm_proj(x, gpre, [(w["conv_w_pw1"], 0), (w["conv_w_pw1"], d)], [(b1, 0), (b1, d)], _glu,
                              [F32], d, PROJ_TM, PROJ_TN, "conformer_pw1")
            u = u.reshape(bsz, t, d)
            v = _cconv(u, conv_buf0, w["conv_w_dw"], _row(w["conv_b_dw"]), _row(w["conv_ln_g"]),
                       _row(w["conv_ln_b"]), CCONV_TC)
            new_conv = jnp.concatenate([conv_buf0, u], axis=1)[:, t:]
            x = _proj_res(v.reshape(bsz * t, d), w["conv_w_pw2"], _row(w["conv_b_pw2"]), x, gpost, OUT_TM,
                          "conformer_pw2")
        else:
            q, k, v = _norm_proj(x, gpre, [(w["attn_w_q"], 0), (w["attn_w_k"], 0), (w["attn_w_v"], 0)], [],
                                 _identity, [F32, F32, F32], d, PROJ_TM, PROJ_TN, "attn_qkv")
            q3, k3, v3 = (a.reshape(bsz, t, d) for a in (q, k, v))
            lams = [_row(w[n]) for n in ("attn_lambda_q1", "attn_lambda_k1", "attn_lambda_q2", "attn_lambda_k2")]
            g = _row(w["attn_subln_g"])
            if past is None:
                o = _flash_diff_attn(q3, k3, v3, lams, g, FLASH_TQ)
            else:
                cache_k, cache_v, page_table = past
                o = _paged_diff_attn(q3, k3, v3, cache_k, cache_v, page_table, lams, g, PAGES_PER_STEP)
            new_k = k3.reshape(bsz, t, ATTN_HEADS, 2 * HEAD_DIM)
            new_v = v3.reshape(bsz, t, ATTN_HEADS, 2 * HEAD_DIM)
            x = _proj_res(o.reshape(bsz * t, d), w["attn_w_o"], None, x, gpost, OUT_TM, "attn_out")
        x = _ffn(x, _row(w["norm_pre"][layer, 2]), w["ffn_wg"], w["ffn_wu"], w["ffn_wd"],
                 _row(w["norm_post"][layer, 2]), layer, 1, FFN_TM, FFN_TF)
    return x.reshape(bsz, t, d), new_k, new_v, jnp.stack(new_h), jnp.stack(new_buf), new_conv


def kernel(x_prompt, x_sample, cache_k, cache_v, page_table, state_lru_h, state_lru_conv, state_conv, norm_pre, norm_post, ffn_w_gate, ffn_w_up, ffn_w_down, lru_w_gate, lru_w_in, lru_conv_w, lru_conv_b, lru_w_a, lru_b_a, lru_w_x, lru_b_x, lru_lambda, lru_w_out, conv_w_pw1, conv_b_pw1, conv_w_dw, conv_b_dw, conv_ln_g, conv_ln_b, conv_w_pw2, conv_b_pw2, attn_w_q, attn_w_k, attn_w_v, attn_w_o, attn_lambda_q1, attn_lambda_k1, attn_lambda_q2, attn_lambda_k2, attn_subln_g):
    w = dict(
        norm_pre=norm_pre, norm_post=norm_post,
        ffn_wg=ffn_w_gate.astype(BF16), ffn_wu=ffn_w_up.astype(BF16), ffn_wd=ffn_w_down.astype(BF16),
        lru_w_gate=lru_w_gate.astype(BF16), lru_w_in=lru_w_in.astype(BF16),
        lru_conv_w=lru_conv_w, lru_conv_b=lru_conv_b,
        lru_w_a=lru_w_a.astype(BF16), lru_b_a=lru_b_a, lru_w_x=lru_w_x.astype(BF16), lru_b_x=lru_b_x,
        lru_lambda=lru_lambda, lru_w_out=lru_w_out.astype(BF16),
        conv_w_pw1=conv_w_pw1.astype(BF16), conv_b_pw1=conv_b_pw1, conv_w_dw=conv_w_dw, conv_b_dw=conv_b_dw,
        conv_ln_g=conv_ln_g, conv_ln_b=conv_ln_b, conv_w_pw2=conv_w_pw2.astype(BF16), conv_b_pw2=conv_b_pw2,
        attn_w_q=attn_w_q.astype(BF16), attn_w_k=attn_w_k.astype(BF16), attn_w_v=attn_w_v.astype(BF16),
        attn_w_o=attn_w_o.astype(BF16),
        attn_lambda_q1=attn_lambda_q1, attn_lambda_k1=attn_lambda_k1,
        attn_lambda_q2=attn_lambda_q2, attn_lambda_k2=attn_lambda_k2, attn_subln_g=attn_subln_g,
    )
    bp, s, d = x_prompt.shape
    db, ds, _ = x_sample.shape
    n_lru = state_lru_h.shape[0]
    r = state_lru_h.shape[-1]

    y_p, k_p, v_p, h_p, lbuf_p, cbuf_p = _trunk(
        x_prompt.reshape(bp * s, d), bp, s,
        jnp.zeros((n_lru, bp, r), F32), jnp.zeros((n_lru, bp, LRU_CONV - 1, r), F32),
        jnp.zeros((bp, CONV_WIDTH - 1, d), F32), None, w)

    n_pool = cache_k.shape[0]
    past = (cache_k.reshape(n_pool, PAGE_SIZE, d), cache_v.reshape(n_pool, PAGE_SIZE, d), page_table)
    y_s, k_s, v_s, h_s, lbuf_s, cbuf_s = _trunk(
        x_sample.reshape(db * ds, d), db, ds, state_lru_h, state_lru_conv, state_conv, past, w)

    return (y_p, y_s, k_p, v_p, h_p, lbuf_p, cbuf_p, k_s, v_s, h_s, lbuf_s, cbuf_s)
```

```python
import functools
import math

import jax
import jax.numpy as jnp
from jax import lax
from jax.experimental import pallas as pl
from jax.experimental.pallas import tpu as pltpu

F32 = jnp.float32
BF16 = jnp.bfloat16

LRU_BLOCK = 256
LRU_C = 8.0
LRU_CONV = 4
CONV_WIDTH = 31
ATTN_HEADS = 8
HEAD_DIM = 128
ATTN_LAYER = 2
LAMBDA_INIT = 0.8 - 0.6 * math.exp(-0.3 * ATTN_LAYER)
NEG_INF = -1e30
PAGE_SIZE = 128

V7X_VMEM_LIMIT_BYTES = 56 * 1024 * 1024
SUBLANES = 8
LANES = 128


def _params(semantics):
    return pltpu.CompilerParams(dimension_semantics=semantics, vmem_limit_bytes=V7X_VMEM_LIMIT_BYTES)


def _rms(x, g, eps):
    return x * lax.rsqrt(jnp.mean(x * x, axis=-1, keepdims=True) + eps) * g


def _nt_dot(a, b):
    return lax.dot_general(a, b, (((1,), (1,)), ((), ())), preferred_element_type=F32)


def _lambda_full(lq1, lk1, lq2, lk2):
    s1 = jnp.sum(lq1 * lk1, axis=-1, keepdims=True)
    s2 = jnp.sum(lq2 * lk2, axis=-1, keepdims=True)
    return jnp.exp(s1) - jnp.exp(s2) + LAMBDA_INIT


def _ffn_kernel(x_ref, gpre_ref, wg_ref, wu_ref, wd_ref, gpost_ref, o_ref, h_ref, *, nj):
    j = pl.program_id(1)

    @pl.when(j == 0)
    def _():
        h_ref[...] = _rms(x_ref[...], gpre_ref[...], 1e-6).astype(BF16)
        o_ref[...] = jnp.zeros(o_ref.shape, F32)

    h = h_ref[...]
    g = jnp.dot(h, wg_ref[...], preferred_element_type=F32)
    u = jnp.dot(h, wu_ref[...], preferred_element_type=F32)
    a = (g * jax.nn.sigmoid(g) * u).astype(BF16)
    o_ref[...] += jnp.dot(a, wd_ref[...], preferred_element_type=F32)

    @pl.when(j == nj - 1)
    def _():
        o_ref[...] = x_ref[...] + 0.5 * _rms(o_ref[...], gpost_ref[...], 1e-6)


def _ffn(x, gpre, wg, wu, wd, gpost, layer, half, tm, tf):
    n, d = x.shape
    f = wg.shape[-1]
    tm = min(tm, n)
    nj = f // tf
    return pl.pallas_call(
        functools.partial(_ffn_kernel, nj=nj),
        out_shape=jax.ShapeDtypeStruct((n, d), F32),
        grid=(n // tm, nj),
        in_specs=[
            pl.BlockSpec((tm, d), lambda i, j: (i, 0)),
            pl.BlockSpec((1, d), lambda i, j: (0, 0)),
            pl.BlockSpec((None, None, d, tf), lambda i, j: (layer, half, 0, j)),
            pl.BlockSpec((None, None, d, tf), lambda i, j: (layer, half, 0, j)),
            pl.BlockSpec((None, None, tf, d), lambda i, j: (layer, half, j, 0)),
            pl.BlockSpec((1, d), lambda i, j: (0, 0)),
        ],
        out_specs=pl.BlockSpec((tm, d), lambda i, j: (i, 0)),
        scratch_shapes=[pltpu.VMEM((tm, d), BF16)],
        compiler_params=_params(("parallel", "arbitrary")),
        name="ffn",
    )(x, gpre, wg, wu, wd, gpost)


def _norm_proj_kernel(*refs, nw, nb, epilogue):
    x_ref, g_ref = refs[0], refs[1]
    w_refs = refs[2:2 + nw]
    b_refs = refs[2 + nw:2 + nw + nb]
    o_refs = refs[2 + nw + nb:-1]
    h_ref = refs[-1]

    @pl.when(pl.program_id(1) == 0)
    def _():
        h_ref[...] = _rms(x_ref[...], g_ref[...], 1e-6).astype(BF16)

    h = h_ref[...]
    accs = [jnp.dot(h, w[...], preferred_element_type=F32) for w in w_refs]
    if nb:
        accs = [a + b[...] for a, b in zip(accs, b_refs)]
    for o_ref, o in zip(o_refs, epilogue(accs)):
        o_ref[...] = o.astype(o_ref.dtype)


def _norm_proj(x, gpre, ws, bs, epilogue, out_dtypes, m, tm, tn, name):
    n, d = x.shape
    tm = min(tm, n)
    w_specs = [pl.BlockSpec((d, tn), functools.partial(lambda i, j, o: (0, j + o), o=off // tn)) for _, off in ws]
    b_specs = [pl.BlockSpec((1, tn), functools.partial(lambda i, j, o: (0, j + o), o=off // tn)) for _, off in bs]
    return pl.pallas_call(
        functools.partial(_norm_proj_kernel, nw=len(ws), nb=len(bs), epilogue=epilogue),
        out_shape=[jax.ShapeDtypeStruct((n, m), dt) for dt in out_dtypes],
        grid=(n // tm, m // tn),
        in_specs=[pl.BlockSpec((tm, d), lambda i, j: (i, 0)), pl.BlockSpec((1, d), lambda i, j: (0, 0))]
        + w_specs + b_specs,
        out_specs=[pl.BlockSpec((tm, tn), lambda i, j: (i, j)) for _ in out_dtypes],
        scratch_shapes=[pltpu.VMEM((tm, d), BF16)],
        compiler_params=_params(("parallel", "arbitrary")),
        name=name,
    )(x, gpre, *[w for w, _ in ws], *[b for b, _ in bs])


def _proj_res_kernel(*refs, has_bias):
    if has_bias:
        a_ref, w_ref, b_ref, x_ref, g_ref, o_ref = refs
    else:
        a_ref, w_ref, x_ref, g_ref, o_ref = refs
    y = jnp.dot(a_ref[...].astype(BF16), w_ref[...], preferred_element_type=F32)
    if has_bias:
        y = y + b_ref[...]
    o_ref[...] = x_ref[...] + _rms(y, g_ref[...], 1e-6)


def _proj_res(a, w, b, x, gpost, tm, name):
    n, k = a.shape
    d = w.shape[1]
    tm = min(tm, n)
    has_bias = b is not None
    in_specs = [pl.BlockSpec((tm, k), lambda i: (i, 0)), pl.BlockSpec((k, d), lambda i: (0, 0))]
    args = [a, w]
    if has_bias:
        in_specs.append(pl.BlockSpec((1, d), lambda i: (0, 0)))
        args.append(b)
    in_specs += [pl.BlockSpec((tm, d), lambda i: (i, 0)), pl.BlockSpec((1, d), lambda i: (0, 0))]
    args += [x, gpost]
    return pl.pallas_call(
        functools.partial(_proj_res_kernel, has_bias=has_bias),
        out_shape=jax.ShapeDtypeStruct((n, d), F32),
        grid=(n // tm,),
        in_specs=in_specs,
        out_specs=pl.BlockSpec((tm, d), lambda i: (i, 0)),
        compiler_params=_params(("parallel",)),
        name=name,
    )(*args)


def _lru_kernel(u0_ref, gate_ref, buf0_ref, h0_ref, cw_ref, cb_ref, wa_ref, ba_ref, wx_ref, bx_ref, lam_ref,
                y_ref, ht_ref, xs_ref, h_ref, *, tc, nt):
    t = pl.program_id(2)
    halo = LRU_CONV - 1

    @pl.when(t == 0)
    def _():
        xs_ref[SUBLANES - halo:SUBLANES, :] = buf0_ref[...]
        h_ref[...] = h0_ref[...]

    x = u0_ref[...]
    xs_ref[SUBLANES:SUBLANES + tc, :] = x
    cw = cw_ref[...]
    u = cw[halo:halo + 1] * x + cb_ref[...]
    for s in range(1, LRU_CONV):
        u = u + cw[halo - s:halo - s + 1] * xs_ref[SUBLANES - s:SUBLANES - s + tc, :]
    xs_ref[0:SUBLANES, :] = xs_ref[tc:tc + SUBLANES, :]

    ub = u.astype(BF16)
    r = jax.nn.sigmoid(jnp.dot(ub, wa_ref[...], preferred_element_type=F32) + ba_ref[...])
    i = jax.nn.sigmoid(jnp.dot(ub, wx_ref[...], preferred_element_type=F32) + bx_ref[...])
    lam = lam_ref[...]
    softplus_neg_lam = jnp.maximum(-lam, 0.0) + jnp.log1p(jnp.exp(-jnp.abs(lam)))
    log_a = -LRU_C * r * softplus_neg_lam
    a = jnp.exp(log_a)
    th = jnp.tanh(log_a)
    b = jnp.sqrt(-2.0 * th / (1.0 - th)) * i * u

    row = lax.broadcasted_iota(jnp.int32, (tc, LRU_BLOCK), 0)
    s = 1
    while s < tc:
        a_sh = jnp.where(row >= s, pltpu.roll(a, s, axis=0), 1.0)
        b_sh = jnp.where(row >= s, pltpu.roll(b, s, axis=0), 0.0)
        b = a * b_sh + b
        a = a * a_sh
        s *= 2
    hs = a * h_ref[...] + b
    h_ref[...] = hs[tc - 1:tc, :]
    y_ref[...] = (hs * gate_ref[...]).astype(y_ref.dtype)

    @pl.when(t == nt - 1)
    def _():
        ht_ref[...] = hs[tc - 1:tc, :]


def _lru_scan(u0, gate, buf0, h0, cw, cb, wa, ba, wx, bx, lam, tc):
    bsz, t, r = u0.shape
    tc = min(tc, t)
    nt = t // tc
    nb = r // LRU_BLOCK
    vec = pl.BlockSpec((1, LRU_BLOCK), lambda b, n, k: (0, n))
    return pl.pallas_call(
        functools.partial(_lru_kernel, tc=tc, nt=nt),
        out_shape=[jax.ShapeDtypeStruct((bsz, t, r), BF16), jax.ShapeDtypeStruct((bsz, 1, r), F32)],
        grid=(bsz, nb, nt),
        in_specs=[
            pl.BlockSpec((None, tc, LRU_BLOCK), lambda b, n, k: (b, k, n)),
            pl.BlockSpec((None, tc, LRU_BLOCK), lambda b, n, k: (b, k, n)),
            pl.BlockSpec((None, LRU_CONV - 1, LRU_BLOCK), lambda b, n, k: (b, 0, n)),
            pl.BlockSpec((None, 1, LRU_BLOCK), lambda b, n, k: (b, 0, n)),
            pl.BlockSpec((LRU_CONV, LRU_BLOCK), lambda b, n, k: (0, n)),
            vec,
            pl.BlockSpec((None, LRU_BLOCK, LRU_BLOCK), lambda b, n, k: (n, 0, 0)),
            vec,
            pl.BlockSpec((None, LRU_BLOCK, LRU_BLOCK), lambda b, n, k: (n, 0, 0)),
            vec,
            vec,
        ],
        out_specs=[
            pl.BlockSpec((None, tc, LRU_BLOCK), lambda b, n, k: (b, k, n)),
            pl.BlockSpec((None, 1, LRU_BLOCK), lambda b, n, k: (b, 0, n)),
        ],
        scratch_shapes=[pltpu.VMEM((tc + SUBLANES, LRU_BLOCK), F32), pltpu.VMEM((1, LRU_BLOCK), F32)],
        compiler_params=_params(("parallel", "parallel", "arbitrary")),
        name="lru_scan",
    )(u0, gate, buf0, h0, cw, cb, wa, ba, wx, bx, lam)


CONV_HALO = 32
CONV_LANES = 256


def _cconv_kernel(u_ref, buf0_ref, w_ref, b_ref, lng_ref, lnb_ref, o_ref, xs_ref, acc_ref, *, tc):
    t = pl.program_id(1)
    d = u_ref.shape[-1]
    pad = CONV_HALO - (CONV_WIDTH - 1)

    @pl.when(t == 0)
    def _():
        xs_ref[pad:CONV_HALO, :] = buf0_ref[...]

    xs_ref[CONV_HALO:CONV_HALO + tc, :] = u_ref[...]
    for c in range(d // CONV_LANES):
        cs = slice(c * CONV_LANES, (c + 1) * CONV_LANES)
        acc = b_ref[:, cs] + w_ref[0:1, cs] * xs_ref[pad:pad + tc, cs]
        for k in range(1, CONV_WIDTH):
            acc = acc + w_ref[k:k + 1, cs] * xs_ref[pad + k:pad + k + tc, cs]
        acc_ref[:, cs] = acc
    xs_ref[0:CONV_HALO, :] = xs_ref[tc:tc + CONV_HALO, :]

    v = acc_ref[...]
    mu = jnp.mean(v, axis=-1, keepdims=True)
    vc = v - mu
    var = jnp.mean(vc * vc, axis=-1, keepdims=True)
    y = vc * lax.rsqrt(var + 1e-5) * lng_ref[...] + lnb_ref[...]
    o_ref[...] = (y * jax.nn.sigmoid(y)).astype(o_ref.dtype)


def _cconv(u, buf0, w, b, lng, lnb, tc):
    bsz, t, d = u.shape
    tc = min(tc, t)
    vec = pl.BlockSpec((1, d), lambda bb, k: (0, 0))
    return pl.pallas_call(
        functools.partial(_cconv_kernel, tc=tc),
        out_shape=jax.ShapeDtypeStruct((bsz, t, d), BF16),
        grid=(bsz, t // tc),
        in_specs=[
            pl.BlockSpec((None, tc, d), lambda bb, k: (bb, k, 0)),
            pl.BlockSpec((None, CONV_WIDTH - 1, d), lambda bb, k: (bb, 0, 0)),
            pl.BlockSpec((CONV_WIDTH, d), lambda bb, k: (0, 0)),
            vec, vec, vec,
        ],
        out_specs=pl.BlockSpec((None, tc, d), lambda bb, k: (bb, k, 0)),
        scratch_shapes=[pltpu.VMEM((tc + CONV_HALO, d), F32), pltpu.VMEM((tc, d), F32)],
        compiler_params=_params(("parallel", "arbitrary")),
        name="conformer_conv",
    )(u, buf0, w, b, lng, lnb)


QK_LOG2_SCALE = HEAD_DIM ** -0.5 * math.log2(math.e)


def _diff_finish(o1, o2, lam, g):
    o = o1 - lam * o2
    return _rms(o, g, 1e-5) * (1.0 - LAMBDA_INIT)


def _flash_kernel(qi_ref, ki_ref, lq1_ref, lk1_ref, lq2_ref, lk2_ref, g_ref, q_ref, k_ref, v_ref, o_ref,
                  m_ref, l_ref, acc_ref, *, tq):
    step = pl.program_id(2)
    qi = qi_ref[step]
    ki = ki_ref[step]
    nkc = tq // LANES

    @pl.when(ki == 0)
    def _():
        m_ref[...] = jnp.full(m_ref.shape, NEG_INF, F32)
        l_ref[...] = jnp.zeros(l_ref.shape, F32)
        acc_ref[...] = jnp.zeros(acc_ref.shape, F32)

    def update(masked):
        q = q_ref[...]
        k = k_ref[...]
        v = v_ref[...]
        if masked:
            keep = (lax.broadcasted_iota(jnp.int32, (tq, tq), 0) >= lax.broadcasted_iota(jnp.int32, (tq, tq), 1))
        for c in range(2):
            cs = slice(c * HEAD_DIM, (c + 1) * HEAD_DIM)
            s = _nt_dot(q[:, cs], k[:, cs])
            if masked:
                s = jnp.where(keep, s, NEG_INF)
            m_prev = m_ref[c]
            m_new = jnp.maximum(m_prev, jnp.max(s, axis=-1, keepdims=True))
            alpha = jnp.exp2(m_prev - m_new)
            ps = [jnp.exp2(s[:, j * LANES:(j + 1) * LANES] - m_new) for j in range(nkc)]
            l_new = alpha * l_ref[c]
            for p in ps:
                l_new = l_new + p
            l_ref[c] = l_new
            m_ref[c] = m_new
            p = jnp.concatenate(ps, axis=1).astype(BF16)
            acc_ref[c] = jnp.concatenate([alpha, alpha], axis=1) * acc_ref[c] + jnp.dot(
                p, v, preferred_element_type=F32)

    @pl.when(ki < qi)
    def _():
        update(False)

    @pl.when(ki == qi)
    def _():
        update(True)
        lam = _lambda_full(lq1_ref[...], lk1_ref[...], lq2_ref[...], lk2_ref[...])
        o1 = acc_ref[0] / jnp.sum(l_ref[0], axis=-1, keepdims=True)
        o2 = acc_ref[1] / jnp.sum(l_ref[1], axis=-1, keepdims=True)
        o_ref[...] = _diff_finish(o1, o2, lam, g_ref[...]).astype(o_ref.dtype)


def _flash_diff_attn(q, k, v, lams, g, tq):
    bsz, s, _ = q.shape
    hd2 = 2 * HEAD_DIM
    nq = s // tq
    pairs = [(i, j) for i in range(nq) for j in range(i + 1)]
    qi_tab = jnp.asarray([p[0] for p in pairs], jnp.int32)
    ki_tab = jnp.asarray([p[1] for p in pairs], jnp.int32)
    small = pl.BlockSpec((1, HEAD_DIM), lambda b, h, t, qt, kt: (0, 0))
    q_spec = pl.BlockSpec((None, tq, hd2), lambda b, h, t, qt, kt: (b, qt[t], h))
    kv_spec = pl.BlockSpec((None, tq, hd2), lambda b, h, t, qt, kt: (b, kt[t], h))
    grid_spec = pltpu.PrefetchScalarGridSpec(
        num_scalar_prefetch=2,
        grid=(bsz, ATTN_HEADS, len(pairs)),
        in_specs=[small, small, small, small, pl.BlockSpec((1, hd2), lambda b, h, t, qt, kt: (0, 0)),
                  q_spec, kv_spec, kv_spec],
        out_specs=q_spec,
        scratch_shapes=[pltpu.VMEM((2, tq, LANES), F32), pltpu.VMEM((2, tq, LANES), F32),
                        pltpu.VMEM((2, tq, hd2), F32)],
    )
    return pl.pallas_call(
        functools.partial(_flash_kernel, tq=tq),
        out_shape=jax.ShapeDtypeStruct(q.shape, BF16),
        grid_spec=grid_spec,
        compiler_params=_params(("parallel", "parallel", "arbitrary")),
        name="flash_diff_attn",
    )(qi_tab, ki_tab, *lams, g, q, k, v)


def _paged_kernel(pt_ref, lq1_ref, lk1_ref, lq2_ref, lk2_ref, g_ref, q_ref, kn_ref, vn_ref, *rest, pp, nj):
    del pt_ref
    k_refs = rest[:pp]
    v_refs = rest[pp:2 * pp]
    o_ref = rest[2 * pp]
    wq_ref, bias_ref, m_ref, l_ref, acc_ref = rest[2 * pp + 1:]
    j = pl.program_id(1)
    t = q_ref.shape[0]
    rows = 2 * ATTN_HEADS * t
    hd2 = 2 * HEAD_DIM
    nkeys = PAGE_SIZE * ATTN_HEADS

    @pl.when(j == 0)
    def _():
        q = q_ref[...]
        lane = lax.broadcasted_iota(jnp.int32, (t, hd2), 1)
        pieces = []
        for h in range(ATTN_HEADS):
            qh = q[:, h * hd2:(h + 1) * hd2]
            pieces.append(jnp.where(lane < HEAD_DIM, qh, 0.0))
            pieces.append(jnp.where(lane >= HEAD_DIM, qh, 0.0))
        wq_ref[...] = jnp.concatenate(pieces, axis=0).astype(BF16)
        row_head = lax.broadcasted_iota(jnp.int32, (rows, nkeys), 0) // (2 * t)
        key_head = lax.broadcasted_iota(jnp.int32, (rows, nkeys), 1) % ATTN_HEADS
        bias_ref[...] = jnp.where(row_head == key_head, 0.0, NEG_INF)
        m_ref[...] = jnp.full(m_ref.shape, NEG_INF, F32)
        l_ref[...] = jnp.zeros(l_ref.shape, F32)
        acc_ref[...] = jnp.zeros(acc_ref.shape, F32)

    def update(k2, v2, bias):
        s = _nt_dot(wq_ref[...], k2.astype(BF16)) + bias
        m_prev = m_ref[...]
        m_new = jnp.maximum(m_prev, jnp.max(s, axis=-1, keepdims=True))
        alpha = jnp.exp2(m_prev - m_new)
        p = jnp.exp2(s - m_new)
        l_ref[...] = alpha * l_ref[...] + jnp.sum(p, axis=-1, keepdims=True)
        m_ref[...] = m_new
        acc_ref[...] = alpha * acc_ref[...] + jnp.dot(p.astype(BF16), v2.astype(BF16),
                                                      preferred_element_type=F32)

    for kr, vr in zip(k_refs, v_refs):
        update(kr[...].reshape(nkeys, hd2), vr[...].reshape(nkeys, hd2), bias_ref[...])

    @pl.when(j == nj - 1)
    def _():
        n_new = t * ATTN_HEADS
        row = lax.broadcasted_iota(jnp.int32, (rows, n_new), 0)
        col = lax.broadcasted_iota(jnp.int32, (rows, n_new), 1)
        keep = (row // (2 * t) == col % ATTN_HEADS) & (col // ATTN_HEADS <= row % t)
        update(kn_ref[...].reshape(n_new, hd2), vn_ref[...].reshape(n_new, hd2), jnp.where(keep, 0.0, NEG_INF))
        lam = _lambda_full(lq1_ref[...], lk1_ref[...], lq2_ref[...], lk2_ref[...])
        o = acc_ref[...] / l_ref[...]
        for h in range(ATTN_HEADS):
            o1 = o[h * 2 * t:h * 2 * t + t]
            o2 = o[h * 2 * t + t:(h + 1) * 2 * t]
            o_ref[:, h * hd2:(h + 1) * hd2] = _diff_finish(o1, o2, lam, g_ref[...]).astype(o_ref.dtype)


def _paged_diff_attn(q, k_new, v_new, cache_k, cache_v, page_table, lams, g, pp):
    bsz, t, dk = q.shape
    hd2 = 2 * HEAD_DIM
    n_pages = page_table.shape[1]
    nj = n_pages // pp
    rows = 2 * ATTN_HEADS * t
    small = pl.BlockSpec((1, HEAD_DIM), lambda b, j, pt: (0, 0))
    tok = pl.BlockSpec((None, t, ATTN_HEADS, hd2), lambda b, j, pt: (b, 0, 0, 0))
    page_specs = [
        pl.BlockSpec((None, PAGE_SIZE, ATTN_HEADS, hd2),
                     functools.partial(lambda b, j, pt, r: (pt[b, j * pp + r], 0, 0, 0), r=r))
        for r in range(pp)
    ]
    grid_spec = pltpu.PrefetchScalarGridSpec(
        num_scalar_prefetch=1,
        grid=(bsz, nj),
        in_specs=[small, small, small, small, pl.BlockSpec((1, hd2), lambda b, j, pt: (0, 0)),
                  pl.BlockSpec((None, t, dk), lambda b, j, pt: (b, 0, 0)), tok, tok] + page_specs + page_specs,
        out_specs=pl.BlockSpec((None, t, dk), lambda b, j, pt: (b, 0, 0)),
        scratch_shapes=[pltpu.VMEM((rows, hd2), BF16), pltpu.VMEM((rows, PAGE_SIZE * ATTN_HEADS), F32),
                        pltpu.VMEM((rows, 1), F32), pltpu.VMEM((rows, 1), F32), pltpu.VMEM((rows, hd2), F32)],
    )
    return pl.pallas_call(
        functools.partial(_paged_kernel, pp=pp, nj=nj),
        out_shape=jax.ShapeDtypeStruct(q.shape, BF16),
        grid_spec=grid_spec,
        compiler_params=_params(("parallel", "arbitrary")),
        name="paged_diff_attn",
    )(page_table, *lams, g, q, k_new, v_new, *([cache_k] * pp), *([cache_v] * pp))


FFN_TM = 512
FFN_TF = 512
PROJ_TM = 512
PROJ_TN = 512
OUT_TM = 256
LRU_TC = 256
CCONV_TC = 128
FLASH_TQ = 512
PAGES_PER_STEP = 4


def _gelu_gate(accs):
    return [jax.nn.gelu(accs[0], approximate=True), accs[1]]


def _glu(accs):
    return [accs[0] * jax.nn.sigmoid(accs[1])]


def _qkv(accs):
    q, k, v = accs
    return [q * QK_LOG2_SCALE, k, v, k, v]


def _row(v):
    return v.reshape(1, -1)


def _trunk(x, bsz, t, lru_h0, lru_buf0, conv_buf0, past, w):
    d = x.shape[-1]
    new_h, new_buf = [], []
    new_conv = new_k = new_v = None
    li = 0
    for layer in range(4):
        x = _ffn(x, _row(w["norm_pre"][layer, 0]), w["ffn_wg"], w["ffn_wu"], w["ffn_wd"],
                 _row(w["norm_post"][layer, 0]), layer, 0, FFN_TM, FFN_TF)
        gpre = _row(w["norm_pre"][layer, 1])
        gpost = _row(w["norm_post"][layer, 1])
        kind = layer % 3
        if kind == 0:
            r = w["lru_w_gate"].shape[-1]
            gate, u0 = _norm_proj(x, gpre, [(w["lru_w_gate"][li], 0), (w["lru_w_in"][li], 0)], [], _gelu_gate,
                                  [F32, F32], r, PROJ_TM, PROJ_TN, "lru_in")
            u0 = u0.reshape(bsz, t, r)
            y, ht = _lru_scan(u0, gate.reshape(bsz, t, r), lru_buf0[li], lru_h0[li].reshape(bsz, 1, r),
                              w["lru_conv_w"][li], _row(w["lru_conv_b"][li]), w["lru_w_a"][li], _row(w["lru_b_a"][li]),
                              w["lru_w_x"][li], _row(w["lru_b_x"][li]), _row(w["lru_lambda"][li]), LRU_TC)
            new_h.append(ht.reshape(bsz, r))
            new_buf.append(jnp.concatenate([lru_buf0[li], u0], axis=1)[:, t:])
            x = _proj_res(y.reshape(bsz * t, r), w["lru_w_out"][li], None, x, gpost, OUT_TM, "lru_out")
            li += 1
        elif kind == 1:
            b1 = _row(w["conv_b_pw1"])
            (u,) = _norm_proj(x, gpre, [(w["conv_w_pw1"], 0), (w["conv_w_pw1"], d)], [(b1, 0), (b1, d)], _glu,
                              [F32], d, PROJ_TM, PROJ_TN, "conformer_pw1")
            u = u.reshape(bsz, t, d)
            v = _cconv(u, conv_buf0, w["conv_w_dw"], _row(w["conv_b_dw"]), _row(w["conv_ln_g"]),
                       _row(w["conv_ln_b"]), CCONV_TC)
            new_conv = jnp.concatenate([conv_buf0, u], axis=1)[:, t:]
            x = _proj_res(v.reshape(bsz * t, d), w["conv_w_pw2"], _row(w["conv_b_pw2"]), x, gpost, OUT_TM,
                          "conformer_pw2")
        else:
            q_dtype = BF16 if past is None else F32
            qs, k, v, kb, vb = _norm_proj(
                x, gpre, [(w["attn_w_q"], 0), (w["attn_w_k"], 0), (w["attn_w_v"], 0)], [], _qkv,
                [q_dtype, F32, F32, BF16, BF16], d, PROJ_TM, PROJ_TN, "attn_qkv")
            lams = [_row(w[n]) for n in ("attn_lambda_q1", "attn_lambda_k1", "attn_lambda_q2", "attn_lambda_k2")]
            g = _row(w["attn_subln_g"])
            new_k = k.reshape(bsz, t, ATTN_HEADS, 2 * HEAD_DIM)
            new_v = v.reshape(bsz, t, ATTN_HEADS, 2 * HEAD_DIM)
            if past is None:
                o = _flash_diff_attn(qs.reshape(bsz, t, d), kb.reshape(bsz, t, d), vb.reshape(bsz, t, d),
                                     lams, g, FLASH_TQ)
            else:
                cache_k, cache_v, page_table = past
                o = _paged_diff_attn(qs.reshape(bsz, t, d), new_k, new_v, cache_k, cache_v, page_table, lams, g,
                                     PAGES_PER_STEP)
            x = _proj_res(o.reshape(bsz * t, d), w["attn_w_o"], None, x, gpost, OUT_TM, "attn_out")
        x = _ffn(x, _row(w["norm_pre"][layer, 2]), w["ffn_wg"], w["ffn_wu"], w["ffn_wd"],
                 _row(w["norm_post"][layer, 2]), layer, 1, FFN_TM, FFN_TF)
    return x.reshape(bsz, t, d), new_k, new_v, jnp.stack(new_h), jnp.stack(new_buf), new_conv


def kernel(x_prompt, x_sample, cache_k, cache_v, page_table, state_lru_h, state_lru_conv, state_conv, norm_pre, norm_post, ffn_w_gate, ffn_w_up, ffn_w_down, lru_w_gate, lru_w_in, lru_conv_w, lru_conv_b, lru_w_a, lru_b_a, lru_w_x, lru_b_x, lru_lambda, lru_w_out, conv_w_pw1, conv_b_pw1, conv_w_dw, conv_b_dw, conv_ln_g, conv_ln_b, conv_w_pw2, conv_b_pw2, attn_w_q, attn_w_k, attn_w_v, attn_w_o, attn_lambda_q1, attn_lambda_k1, attn_lambda_q2, attn_lambda_k2, attn_subln_g):
    w = dict(
        norm_pre=norm_pre, norm_post=norm_post,
        ffn_wg=ffn_w_gate.astype(BF16), ffn_wu=ffn_w_up.astype(BF16), ffn_wd=ffn_w_down.astype(BF16),
        lru_w_gate=lru_w_gate.astype(BF16), lru_w_in=lru_w_in.astype(BF16),
        lru_conv_w=lru_conv_w, lru_conv_b=lru_conv_b,
        lru_w_a=lru_w_a.astype(BF16), lru_b_a=lru_b_a, lru_w_x=lru_w_x.astype(BF16), lru_b_x=lru_b_x,
        lru_lambda=lru_lambda, lru_w_out=lru_w_out.astype(BF16),
        conv_w_pw1=conv_w_pw1.astype(BF16), conv_b_pw1=conv_b_pw1, conv_w_dw=conv_w_dw, conv_b_dw=conv_b_dw,
        conv_ln_g=conv_ln_g, conv_ln_b=conv_ln_b, conv_w_pw2=conv_w_pw2.astype(BF16), conv_b_pw2=conv_b_pw2,
        attn_w_q=attn_w_q.astype(BF16), attn_w_k=attn_w_k.astype(BF16), attn_w_v=attn_w_v.astype(BF16),
        attn_w_o=attn_w_o.astype(BF16),
        attn_lambda_q1=attn_lambda_q1, attn_lambda_k1=attn_lambda_k1,
        attn_lambda_q2=attn_lambda_q2, attn_lambda_k2=attn_lambda_k2, attn_subln_g=attn_subln_g,
    )
    bp, s, d = x_prompt.shape
    db, ds, _ = x_sample.shape
    n_lru = state_lru_h.shape[0]
    r = state_lru_h.shape[-1]

    y_p, k_p, v_p, h_p, lbuf_p, cbuf_p = _trunk(
        x_prompt.reshape(bp * s, d), bp, s,
        jnp.zeros((n_lru, bp, r), F32), jnp.zeros((n_lru, bp, LRU_CONV - 1, r), F32),
        jnp.zeros((bp, CONV_WIDTH - 1, d), F32), None, w)

    y_s, k_s, v_s, h_s, lbuf_s, cbuf_s = _trunk(
        x_sample.reshape(db * ds, d), db, ds, state_lru_h, state_lru_conv, state_conv,
        (cache_k, cache_v, page_table), w)

    return (y_p, y_s, k_p, v_p, h_p, lbuf_p, cbuf_p, k_s, v_s, h_s, lbuf_s, cbuf_s)
```

```python
import functools
import math

import jax
import jax.numpy as jnp
from jax import lax
from jax.experimental import pallas as pl
from jax.experimental.pallas import tpu as pltpu

F32 = jnp.float32
BF16 = jnp.bfloat16

LRU_BLOCK = 256
LRU_C = 8.0
LRU_CONV = 4
CONV_WIDTH = 31
ATTN_HEADS = 8
HEAD_DIM = 128
ATTN_LAYER = 2
LAMBDA_INIT = 0.8 - 0.6 * math.exp(-0.3 * ATTN_LAYER)
NEG_INF = -1e30
PAGE_SIZE = 128

V7X_VMEM_LIMIT_BYTES = 60 * 1024 * 1024
SUBLANES = 8
LANES = 128


def _params(semantics):
    return pltpu.CompilerParams(dimension_semantics=semantics, vmem_limit_bytes=V7X_VMEM_LIMIT_BYTES)


def _rms(x, g, eps):
    return x * lax.rsqrt(jnp.mean(x * x, axis=-1, keepdims=True) + eps) * g


def _nt_dot(a, b):
    return lax.dot_general(a, b, (((1,), (1,)), ((), ())), preferred_element_type=F32)


def _lambda_full(lq1, lk1, lq2, lk2):
    s1 = jnp.sum(lq1 * lk1, axis=-1, keepdims=True)
    s2 = jnp.sum(lq2 * lk2, axis=-1, keepdims=True)
    return jnp.exp(s1) - jnp.exp(s2) + LAMBDA_INIT


def _ffn_kernel(x_ref, gpre_ref, wg_ref, wu_ref, wd_ref, gpost_ref, o_ref, *rest, nj, emit_bf16):
    h_ref = rest[-1]
    j = pl.program_id(1)

    @pl.when(j == 0)
    def _():
        h_ref[...] = _rms(x_ref[...], gpre_ref[...], 1e-6).astype(BF16)
        o_ref[...] = jnp.zeros(o_ref.shape, F32)

    wg, wu, wd = wg_ref[...], wu_ref[...], wd_ref[...]
    if emit_bf16:
        wg, wu, wd = wg.astype(BF16), wu.astype(BF16), wd.astype(BF16)
        rest[0][...], rest[1][...], rest[2][...] = wg, wu, wd
    h = h_ref[...]
    g = jnp.dot(h, wg, preferred_element_type=F32)
    u = jnp.dot(h, wu, preferred_element_type=F32)
    a = (g * jax.nn.sigmoid(g) * u).astype(BF16)
    o_ref[...] += jnp.dot(a, wd, preferred_element_type=F32)

    @pl.when(j == nj - 1)
    def _():
        o_ref[...] = x_ref[...] + 0.5 * _rms(o_ref[...], gpost_ref[...], 1e-6)


def _ffn(x, gpre, wg, wu, wd, gpost, tm, tf, index=None):
    n, d = x.shape
    f = wg.shape[-1]
    tm = min(tm, n)
    nj = f // tf
    emit_bf16 = index is not None
    if emit_bf16:
        assert n == tm, "weight tiles must be visited once to be written once"
        w_in = [pl.BlockSpec((None, None, d, tf), lambda i, j: (*index, 0, j)),
                pl.BlockSpec((None, None, d, tf), lambda i, j: (*index, 0, j)),
                pl.BlockSpec((None, None, tf, d), lambda i, j: (*index, j, 0))]
    else:
        w_in = [pl.BlockSpec((d, tf), lambda i, j: (0, j)), pl.BlockSpec((d, tf), lambda i, j: (0, j)),
                pl.BlockSpec((tf, d), lambda i, j: (j, 0))]
    out_shape = [jax.ShapeDtypeStruct((n, d), F32)]
    out_specs = [pl.BlockSpec((tm, d), lambda i, j: (i, 0))]
    if emit_bf16:
        out_shape += [jax.ShapeDtypeStruct((d, f), BF16), jax.ShapeDtypeStruct((d, f), BF16),
                      jax.ShapeDtypeStruct((f, d), BF16)]
        out_specs += [pl.BlockSpec((d, tf), lambda i, j: (0, j)), pl.BlockSpec((d, tf), lambda i, j: (0, j)),
                      pl.BlockSpec((tf, d), lambda i, j: (j, 0))]
    outs = pl.pallas_call(
        functools.partial(_ffn_kernel, nj=nj, emit_bf16=emit_bf16),
        out_shape=out_shape,
        grid=(n // tm, nj),
        in_specs=[pl.BlockSpec((tm, d), lambda i, j: (i, 0)), pl.BlockSpec((1, d), lambda i, j: (0, 0))]
        + w_in + [pl.BlockSpec((1, d), lambda i, j: (0, 0))],
        out_specs=out_specs,
        scratch_shapes=[pltpu.VMEM((tm, d), BF16)],
        compiler_params=_params(("parallel", "arbitrary")),
        name="ffn_cast" if emit_bf16 else "ffn",
    )(x, gpre, wg, wu, wd, gpost)
    return outs if emit_bf16 else outs[0]


def _norm_proj_kernel(*refs, nw, nb, epilogue):
    x_ref, g_ref = refs[0], refs[1]
    w_refs = refs[2:2 + nw]
    b_refs = refs[2 + nw:2 + nw + nb]
    o_refs = refs[2 + nw + nb:-1]
    h_ref = refs[-1]

    @pl.when(pl.program_id(1) == 0)
    def _():
        h_ref[...] = _rms(x_ref[...], g_ref[...], 1e-6).astype(BF16)

    h = h_ref[...]
    accs = [jnp.dot(h, w[...], preferred_element_type=F32) for w in w_refs]
    if nb:
        accs = [a + b[...] for a, b in zip(accs, b_refs)]
    for o_ref, o in zip(o_refs, epilogue(accs)):
        o_ref[...] = o.astype(o_ref.dtype)


def _norm_proj(x, gpre, ws, bs, epilogue, out_dtypes, m, tm, tn, name):
    n, d = x.shape
    tm = min(tm, n)
    w_specs = [pl.BlockSpec((d, tn), functools.partial(lambda i, j, o: (0, j + o), o=off // tn)) for _, off in ws]
    b_specs = [pl.BlockSpec((1, tn), functools.partial(lambda i, j, o: (0, j + o), o=off // tn)) for _, off in bs]
    return pl.pallas_call(
        functools.partial(_norm_proj_kernel, nw=len(ws), nb=len(bs), epilogue=epilogue),
        out_shape=[jax.ShapeDtypeStruct((n, m), dt) for dt in out_dtypes],
        grid=(n // tm, m // tn),
        in_specs=[pl.BlockSpec((tm, d), lambda i, j: (i, 0)), pl.BlockSpec((1, d), lambda i, j: (0, 0))]
        + w_specs + b_specs,
        out_specs=[pl.BlockSpec((tm, tn), lambda i, j: (i, j)) for _ in out_dtypes],
        scratch_shapes=[pltpu.VMEM((tm, d), BF16)],
        compiler_params=_params(("parallel", "arbitrary")),
        name=name,
    )(x, gpre, *[w for w, _ in ws], *[b for b, _ in bs])


def _proj_res_kernel(*refs, has_bias):
    if has_bias:
        a_ref, w_ref, b_ref, x_ref, g_ref, o_ref = refs
    else:
        a_ref, w_ref, x_ref, g_ref, o_ref = refs
    y = jnp.dot(a_ref[...].astype(BF16), w_ref[...], preferred_element_type=F32)
    if has_bias:
        y = y + b_ref[...]
    o_ref[...] = x_ref[...] + _rms(y, g_ref[...], 1e-6)


def _proj_res(a, w, b, x, gpost, tm, name):
    n, k = a.shape
    d = w.shape[1]
    tm = min(tm, n)
    has_bias = b is not None
    in_specs = [pl.BlockSpec((tm, k), lambda i: (i, 0)), pl.BlockSpec((k, d), lambda i: (0, 0))]
    args = [a, w]
    if has_bias:
        in_specs.append(pl.BlockSpec((1, d), lambda i: (0, 0)))
        args.append(b)
    in_specs += [pl.BlockSpec((tm, d), lambda i: (i, 0)), pl.BlockSpec((1, d), lambda i: (0, 0))]
    args += [x, gpost]
    return pl.pallas_call(
        functools.partial(_proj_res_kernel, has_bias=has_bias),
        out_shape=jax.ShapeDtypeStruct((n, d), F32),
        grid=(n // tm,),
        in_specs=in_specs,
        out_specs=pl.BlockSpec((tm, d), lambda i: (i, 0)),
        compiler_params=_params(("parallel",)),
        name=name,
    )(*args)


def _lru_kernel(u0_ref, gate_ref, buf0_ref, h0_ref, cw_ref, cb_ref, wa_ref, ba_ref, wx_ref, bx_ref, lam_ref,
                y_ref, ht_ref, xs_ref, h_ref, *, tc, nt):
    t = pl.program_id(2)
    halo = LRU_CONV - 1

    @pl.when(t == 0)
    def _():
        xs_ref[SUBLANES - halo:SUBLANES, :] = buf0_ref[...]
        h_ref[...] = h0_ref[...]

    x = u0_ref[...]
    xs_ref[SUBLANES:SUBLANES + tc, :] = x
    cw = cw_ref[...]
    u = cw[halo:halo + 1] * x + cb_ref[...]
    for s in range(1, LRU_CONV):
        u = u + cw[halo - s:halo - s + 1] * xs_ref[SUBLANES - s:SUBLANES - s + tc, :]
    xs_ref[0:SUBLANES, :] = xs_ref[tc:tc + SUBLANES, :]

    ub = u.astype(BF16)
    r = jax.nn.sigmoid(jnp.dot(ub, wa_ref[...], preferred_element_type=F32) + ba_ref[...])
    i = jax.nn.sigmoid(jnp.dot(ub, wx_ref[...], preferred_element_type=F32) + bx_ref[...])
    lam = lam_ref[...]
    softplus_neg_lam = jnp.maximum(-lam, 0.0) + jnp.log1p(jnp.exp(-jnp.abs(lam)))
    log_a = -LRU_C * r * softplus_neg_lam
    a = jnp.exp(log_a)
    th = jnp.tanh(log_a)
    b = jnp.sqrt(-2.0 * th / (1.0 - th)) * i * u

    row = lax.broadcasted_iota(jnp.int32, (tc, LRU_BLOCK), 0)
    s = 1
    while s < tc:
        a_sh = jnp.where(row >= s, pltpu.roll(a, s, axis=0), 1.0)
        b_sh = jnp.where(row >= s, pltpu.roll(b, s, axis=0), 0.0)
        b = a * b_sh + b
        a = a * a_sh
        s *= 2
    hs = a * h_ref[...] + b
    h_ref[...] = hs[tc - 1:tc, :]
    y_ref[...] = (hs * gate_ref[...]).astype(y_ref.dtype)

    @pl.when(t == nt - 1)
    def _():
        ht_ref[...] = hs[tc - 1:tc, :]


def _lru_scan(u0, gate, buf0, h0, cw, cb, wa, ba, wx, bx, lam, tc):
    bsz, t, r = u0.shape
    tc = min(tc, t)
    nt = t // tc
    nb = r // LRU_BLOCK
    vec = pl.BlockSpec((1, LRU_BLOCK), lambda b, n, k: (0, n))
    return pl.pallas_call(
        functools.partial(_lru_kernel, tc=tc, nt=nt),
        out_shape=[jax.ShapeDtypeStruct((bsz, t, r), BF16), jax.ShapeDtypeStruct((bsz, 1, r), F32)],
        grid=(bsz, nb, nt),
        in_specs=[
            pl.BlockSpec((None, tc, LRU_BLOCK), lambda b, n, k: (b, k, n)),
            pl.BlockSpec((None, tc, LRU_BLOCK), lambda b, n, k: (b, k, n)),
            pl.BlockSpec((None, LRU_CONV - 1, LRU_BLOCK), lambda b, n, k: (b, 0, n)),
            pl.BlockSpec((None, 1, LRU_BLOCK), lambda b, n, k: (b, 0, n)),
            pl.BlockSpec((LRU_CONV, LRU_BLOCK), lambda b, n, k: (0, n)),
            vec,
            pl.BlockSpec((None, LRU_BLOCK, LRU_BLOCK), lambda b, n, k: (n, 0, 0)),
            vec,
            pl.BlockSpec((None, LRU_BLOCK, LRU_BLOCK), lambda b, n, k: (n, 0, 0)),
            vec,
            vec,
        ],
        out_specs=[
            pl.BlockSpec((None, tc, LRU_BLOCK), lambda b, n, k: (b, k, n)),
            pl.BlockSpec((None, 1, LRU_BLOCK), lambda b, n, k: (b, 0, n)),
        ],
        scratch_shapes=[pltpu.VMEM((tc + SUBLANES, LRU_BLOCK), F32), pltpu.VMEM((1, LRU_BLOCK), F32)],
        compiler_params=_params(("parallel", "parallel", "arbitrary")),
        name="lru_scan",
    )(u0, gate, buf0, h0, cw, cb, wa, ba, wx, bx, lam)


def _lru_short_kernel(u0_ref, gate_ref, past_ref, h0_ref, cw_ref, cb_ref, wa_ref, ba_ref, wx_ref, bx_ref, lam_ref,
                      y_ref, ht_ref):
    x = u0_ref[...]
    bsz, t, _ = x.shape
    halo = LRU_CONV - 1
    pos = lax.broadcasted_iota(jnp.int32, x.shape, 1)
    past = past_ref[...]
    cw = cw_ref[...]
    u = cw[halo:halo + 1] * x + cb_ref[...]
    for s in range(1, LRU_CONV):
        shifted = jnp.where(pos >= s, pltpu.roll(x, s, axis=1), pltpu.roll(past, s, axis=1))
        u = u + cw[halo - s:halo - s + 1] * shifted

    u2 = u.reshape(bsz * t, LRU_BLOCK)
    ub = u2.astype(BF16)
    r = jax.nn.sigmoid(jnp.dot(ub, wa_ref[...], preferred_element_type=F32) + ba_ref[...])
    i = jax.nn.sigmoid(jnp.dot(ub, wx_ref[...], preferred_element_type=F32) + bx_ref[...])
    lam = lam_ref[...]
    softplus_neg_lam = jnp.maximum(-lam, 0.0) + jnp.log1p(jnp.exp(-jnp.abs(lam)))
    log_a = -LRU_C * r * softplus_neg_lam
    th = jnp.tanh(log_a)
    a = jnp.exp(log_a).reshape(x.shape)
    b = (jnp.sqrt(-2.0 * th / (1.0 - th)) * i * u2).reshape(x.shape)

    s = 1
    while s < t:
        a_sh = jnp.where(pos >= s, pltpu.roll(a, s, axis=1), 1.0)
        b_sh = jnp.where(pos >= s, pltpu.roll(b, s, axis=1), 0.0)
        b = a * b_sh + b
        a = a * a_sh
        s *= 2
    hs = a * h0_ref[...] + b
    y_ref[...] = (hs * gate_ref[...]).astype(y_ref.dtype)
    ht_ref[...] = hs[:, t - 1:t, :]


def _lru_scan_short(u0, gate, buf0, h0, cw, cb, wa, ba, wx, bx, lam):
    bsz, t, r = u0.shape
    past = jnp.concatenate([jnp.zeros((bsz, t - (LRU_CONV - 1), r), F32), buf0], axis=1)
    seq = pl.BlockSpec((bsz, t, LRU_BLOCK), lambda n: (0, 0, n))
    vec = pl.BlockSpec((1, LRU_BLOCK), lambda n: (0, n))
    mat = pl.BlockSpec((None, LRU_BLOCK, LRU_BLOCK), lambda n: (n, 0, 0))
    state = pl.BlockSpec((bsz, 1, LRU_BLOCK), lambda n: (0, 0, n))
    return pl.pallas_call(
        _lru_short_kernel,
        out_shape=[jax.ShapeDtypeStruct((bsz, t, r), BF16), jax.ShapeDtypeStruct((bsz, 1, r), F32)],
        grid=(r // LRU_BLOCK,),
        in_specs=[seq, seq, seq, state, pl.BlockSpec((LRU_CONV, LRU_BLOCK), lambda n: (0, n)), vec,
                  mat, vec, mat, vec, vec],
        out_specs=[seq, state],
        compiler_params=_params(("parallel",)),
        name="lru_scan_short",
    )(u0, gate, past, h0, cw, cb, wa, ba, wx, bx, lam)


CONV_HALO = 32
CONV_LANES = 256


def _cconv_kernel(u_ref, buf0_ref, w_ref, b_ref, lng_ref, lnb_ref, o_ref, xs_ref, acc_ref, *, tc):
    t = pl.program_id(1)
    d = u_ref.shape[-1]
    pad = CONV_HALO - (CONV_WIDTH - 1)

    @pl.when(t == 0)
    def _():
        xs_ref[pad:CONV_HALO, :] = buf0_ref[...]

    xs_ref[CONV_HALO:CONV_HALO + tc, :] = u_ref[...]
    for c in range(d // CONV_LANES):
        cs = slice(c * CONV_LANES, (c + 1) * CONV_LANES)
        acc = b_ref[:, cs]
        for r in range(SUBLANES):
            span = tc + SUBLANES * ((CONV_WIDTH - 1 - r) // SUBLANES)
            z = xs_ref[pad + r:pad + r + span, cs]
            part = None
            for k in range(r, CONV_WIDTH, SUBLANES):
                term = w_ref[k:k + 1, cs] * z[k - r:k - r + tc]
                part = term if part is None else part + term
            acc = acc + part
        acc_ref[:, cs] = acc
    xs_ref[0:CONV_HALO, :] = xs_ref[tc:tc + CONV_HALO, :]

    v = acc_ref[...]
    mu = jnp.mean(v, axis=-1, keepdims=True)
    vc = v - mu
    var = jnp.mean(vc * vc, axis=-1, keepdims=True)
    y = vc * lax.rsqrt(var + 1e-5) * lng_ref[...] + lnb_ref[...]
    o_ref[...] = (y * jax.nn.sigmoid(y)).astype(o_ref.dtype)


def _cconv(u, buf0, w, b, lng, lnb, tc):
    bsz, t, d = u.shape
    tc = min(tc, t)
    vec = pl.BlockSpec((1, d), lambda bb, k: (0, 0))
    return pl.pallas_call(
        functools.partial(_cconv_kernel, tc=tc),
        out_shape=jax.ShapeDtypeStruct((bsz, t, d), BF16),
        grid=(bsz, t // tc),
        in_specs=[
            pl.BlockSpec((None, tc, d), lambda bb, k: (bb, k, 0)),
            pl.BlockSpec((None, CONV_WIDTH - 1, d), lambda bb, k: (bb, 0, 0)),
            pl.BlockSpec((CONV_WIDTH, d), lambda bb, k: (0, 0)),
            vec, vec, vec,
        ],
        out_specs=pl.BlockSpec((None, tc, d), lambda bb, k: (bb, k, 0)),
        scratch_shapes=[pltpu.VMEM((tc + CONV_HALO, d), F32), pltpu.VMEM((tc, d), F32)],
        compiler_params=_params(("parallel", "arbitrary")),
        name="conformer_conv",
    )(u, buf0, w, b, lng, lnb)


QK_LOG2_SCALE = HEAD_DIM ** -0.5 * math.log2(math.e)


def _diff_finish(o1, o2, lam, g):
    o = o1 - lam * o2
    return _rms(o, g, 1e-5) * (1.0 - LAMBDA_INIT)


def _flash_kernel(qi_ref, ki_ref, lq1_ref, lk1_ref, lq2_ref, lk2_ref, g_ref, q_ref, k_ref, v_ref, o_ref,
                  m_ref, l_ref, acc_ref, *, tq):
    step = pl.program_id(2)
    qi = qi_ref[step]
    ki = ki_ref[step]
    nkc = tq // LANES

    @pl.when(ki == 0)
    def _():
        m_ref[...] = jnp.full(m_ref.shape, NEG_INF, F32)
        l_ref[...] = jnp.zeros(l_ref.shape, F32)
        acc_ref[...] = jnp.zeros(acc_ref.shape, F32)

    def update(masked):
        q = q_ref[...]
        k = k_ref[...]
        v = v_ref[...]
        if masked:
            keep = (lax.broadcasted_iota(jnp.int32, (tq, tq), 0) >= lax.broadcasted_iota(jnp.int32, (tq, tq), 1))
        for c in range(2):
            cs = slice(c * HEAD_DIM, (c + 1) * HEAD_DIM)
            s = _nt_dot(q[:, cs], k[:, cs])
            if masked:
                s = jnp.where(keep, s, NEG_INF)
            m_prev = m_ref[c]
            m_new = jnp.maximum(m_prev, jnp.max(s, axis=-1, keepdims=True))
            alpha = jnp.exp2(m_prev - m_new)
            ps = [jnp.exp2(s[:, j * LANES:(j + 1) * LANES] - m_new) for j in range(nkc)]
            l_new = alpha * l_ref[c]
            for p in ps:
                l_new = l_new + p
            l_ref[c] = l_new
            m_ref[c] = m_new
            p = jnp.concatenate(ps, axis=1).astype(BF16)
            acc_ref[c] = jnp.concatenate([alpha, alpha], axis=1) * acc_ref[c] + jnp.dot(
                p, v, preferred_element_type=F32)

    @pl.when(ki < qi)
    def _():
        update(False)

    @pl.when(ki == qi)
    def _():
        update(True)
        lam = _lambda_full(lq1_ref[...], lk1_ref[...], lq2_ref[...], lk2_ref[...])
        o1 = acc_ref[0] / jnp.sum(l_ref[0], axis=-1, keepdims=True)
        o2 = acc_ref[1] / jnp.sum(l_ref[1], axis=-1, keepdims=True)
        o_ref[...] = _diff_finish(o1, o2, lam, g_ref[...]).astype(o_ref.dtype)


def _flash_diff_attn(q, k, v, lams, g, tq):
    bsz, s, _ = q.shape
    hd2 = 2 * HEAD_DIM
    nq = s // tq
    pairs = [(i, j) for i in range(nq) for j in range(i + 1)]
    qi_tab = jnp.asarray([p[0] for p in pairs], jnp.int32)
    ki_tab = jnp.asarray([p[1] for p in pairs], jnp.int32)
    small = pl.BlockSpec((1, HEAD_DIM), lambda b, h, t, qt, kt: (0, 0))
    q_spec = pl.BlockSpec((None, tq, hd2), lambda b, h, t, qt, kt: (b, qt[t], h))
    kv_spec = pl.BlockSpec((None, tq, hd2), lambda b, h, t, qt, kt: (b, kt[t], h))
    grid_spec = pltpu.PrefetchScalarGridSpec(
        num_scalar_prefetch=2,
        grid=(bsz, ATTN_HEADS, len(pairs)),
        in_specs=[small, small, small, small, pl.BlockSpec((1, hd2), lambda b, h, t, qt, kt: (0, 0)),
                  q_spec, kv_spec, kv_spec],
        out_specs=q_spec,
        scratch_shapes=[pltpu.VMEM((2, tq, LANES), F32), pltpu.VMEM((2, tq, LANES), F32),
                        pltpu.VMEM((2, tq, hd2), F32)],
    )
    return pl.pallas_call(
        functools.partial(_flash_kernel, tq=tq),
        out_shape=jax.ShapeDtypeStruct(q.shape, BF16),
        grid_spec=grid_spec,
        compiler_params=_params(("parallel", "parallel", "arbitrary")),
        name="flash_diff_attn",
    )(qi_tab, ki_tab, *lams, g, q, k, v)


def _paged_kernel(pt_ref, lq1_ref, lk1_ref, lq2_ref, lk2_ref, g_ref, q_ref, kn_ref, vn_ref, *rest, pp, nj):
    del pt_ref
    k_refs = rest[:pp]
    v_refs = rest[pp:2 * pp]
    o_ref = rest[2 * pp]
    wq_ref, bias_ref, m_ref, l_ref, acc_ref = rest[2 * pp + 1:]
    j = pl.program_id(1)
    t = q_ref.shape[0]
    rows = 2 * ATTN_HEADS * t
    hd2 = 2 * HEAD_DIM
    nkeys = PAGE_SIZE * ATTN_HEADS

    @pl.when(j == 0)
    def _():
        q = q_ref[...]
        lane = lax.broadcasted_iota(jnp.int32, (t, hd2), 1)
        pieces = []
        for h in range(ATTN_HEADS):
            qh = q[:, h * hd2:(h + 1) * hd2]
            pieces.append(jnp.where(lane < HEAD_DIM, qh, 0.0))
            pieces.append(jnp.where(lane >= HEAD_DIM, qh, 0.0))
        wq_ref[...] = jnp.concatenate(pieces, axis=0).astype(BF16)
        row_head = lax.broadcasted_iota(jnp.int32, (rows, nkeys), 0) // (2 * t)
        key_head = lax.broadcasted_iota(jnp.int32, (rows, nkeys), 1) % ATTN_HEADS
        bias_ref[...] = jnp.where(row_head == key_head, 0.0, NEG_INF)
        m_ref[...] = jnp.full(m_ref.shape, NEG_INF, F32)
        l_ref[...] = jnp.zeros(l_ref.shape, F32)
        acc_ref[...] = jnp.zeros(acc_ref.shape, F32)

    def update(k2s, v2s, biases):
        wq = wq_ref[...]
        ss = [_nt_dot(wq, k2.astype(BF16)) + bias for k2, bias in zip(k2s, biases)]
        m_prev = m_ref[...]
        m_new = m_prev
        for s in ss:
            m_new = jnp.maximum(m_new, jnp.max(s, axis=-1, keepdims=True))
        alpha = jnp.exp2(m_prev - m_new)
        ps = [jnp.exp2(s - m_new) for s in ss]
        l_new = alpha * l_ref[...]
        pv = None
        for p, v2 in zip(ps, v2s):
            l_new = l_new + jnp.sum(p, axis=-1, keepdims=True)
            d = jnp.dot(p.astype(BF16), v2.astype(BF16), preferred_element_type=F32)
            pv = d if pv is None else pv + d
        l_ref[...] = l_new
        m_ref[...] = m_new
        acc_ref[...] = alpha * acc_ref[...] + pv

    update([kr[...].reshape(nkeys, hd2) for kr in k_refs], [vr[...].reshape(nkeys, hd2) for vr in v_refs],
           [bias_ref[...]] * pp)

    @pl.when(j == nj - 1)
    def _():
        n_new = t * ATTN_HEADS
        row = lax.broadcasted_iota(jnp.int32, (rows, n_new), 0)
        col = lax.broadcasted_iota(jnp.int32, (rows, n_new), 1)
        keep = (row // (2 * t) == col % ATTN_HEADS) & (col // ATTN_HEADS <= row % t)
        update([kn_ref[...].reshape(n_new, hd2)], [vn_ref[...].reshape(n_new, hd2)],
               [jnp.where(keep, 0.0, NEG_INF)])
        lam = _lambda_full(lq1_ref[...], lk1_ref[...], lq2_ref[...], lk2_ref[...])
        o = acc_ref[...] / l_ref[...]
        for h in range(ATTN_HEADS):
            o1 = o[h * 2 * t:h * 2 * t + t]
            o2 = o[h * 2 * t + t:(h + 1) * 2 * t]
            o_ref[:, h * hd2:(h + 1) * hd2] = _diff_finish(o1, o2, lam, g_ref[...]).astype(o_ref.dtype)


def _paged_diff_attn(q, k_new, v_new, cache_k, cache_v, page_table, lams, g, pp):
    bsz, t, dk = q.shape
    hd2 = 2 * HEAD_DIM
    n_pages = page_table.shape[1]
    nj = n_pages // pp
    rows = 2 * ATTN_HEADS * t
    small = pl.BlockSpec((1, HEAD_DIM), lambda b, j, pt: (0, 0))
    tok = pl.BlockSpec((None, t, ATTN_HEADS, hd2), lambda b, j, pt: (b, 0, 0, 0))
    page_specs = [
        pl.BlockSpec((None, PAGE_SIZE, ATTN_HEADS, hd2),
                     functools.partial(lambda b, j, pt, r: (pt[b, j * pp + r], 0, 0, 0), r=r))
        for r in range(pp)
    ]
    grid_spec = pltpu.PrefetchScalarGridSpec(
        num_scalar_prefetch=1,
        grid=(bsz, nj),
        in_specs=[small, small, small, small, pl.BlockSpec((1, hd2), lambda b, j, pt: (0, 0)),
                  pl.BlockSpec((None, t, dk), lambda b, j, pt: (b, 0, 0)), tok, tok] + page_specs + page_specs,
        out_specs=pl.BlockSpec((None, t, dk), lambda b, j, pt: (b, 0, 0)),
        scratch_shapes=[pltpu.VMEM((rows, hd2), BF16), pltpu.VMEM((rows, PAGE_SIZE * ATTN_HEADS), F32),
                        pltpu.VMEM((rows, 1), F32), pltpu.VMEM((rows, 1), F32), pltpu.VMEM((rows, hd2), F32)],
    )
    return pl.pallas_call(
        functools.partial(_paged_kernel, pp=pp, nj=nj),
        out_shape=jax.ShapeDtypeStruct(q.shape, BF16),
        grid_spec=grid_spec,
        compiler_params=_params(("parallel", "arbitrary")),
        name="paged_diff_attn",
    )(page_table, *lams, g, q, k_new, v_new, *([cache_k] * pp), *([cache_v] * pp))


FFN_TM = 512
FFN_TF = 512
PROJ_TM = 1024
PROJ_TN = 512
OUT_TM = 512
LRU_TC = 256
CCONV_TC = 128
FLASH_TQ = 512
PAGES_PER_STEP = 8


def _gelu_gate(accs):
    return [jax.nn.gelu(accs[0], approximate=True), accs[1]]


def _glu(accs):
    return [accs[0] * jax.nn.sigmoid(accs[1])]


def _qkv(accs):
    q, k, v = accs
    return [q * QK_LOG2_SCALE, k, v, k, v]


def _row(v):
    return v.reshape(1, -1)


def _trunk(x, bsz, t, lru_h0, lru_buf0, conv_buf0, past, w, ffn_bf16):
    d = x.shape[-1]
    new_h, new_buf = [], []
    new_conv = new_k = new_v = None
    li = 0

    def ffn(x, layer, half):
        gpre, gpost = _row(w["norm_pre"][layer, 2 * half]), _row(w["norm_post"][layer, 2 * half])
        if (layer, half) in ffn_bf16:
            return _ffn(x, gpre, *ffn_bf16[layer, half], gpost, FFN_TM, FFN_TF)
        x, *ffn_bf16[layer, half] = _ffn(x, gpre, w["ffn_w_gate"], w["ffn_w_up"], w["ffn_w_down"], gpost,
                                         FFN_TM, FFN_TF, index=(layer, half))
        return x

    for layer in range(4):
        x = ffn(x, layer, 0)
        gpre = _row(w["norm_pre"][layer, 1])
        gpost = _row(w["norm_post"][layer, 1])
        kind = layer % 3
        if kind == 0:
            r = w["lru_w_gate"].shape[-1]
            gate, u0 = _norm_proj(x, gpre, [(w["lru_w_gate"][li], 0), (w["lru_w_in"][li], 0)], [], _gelu_gate,
                                  [F32, F32], r, PROJ_TM, PROJ_TN, "lru_in")
            u0 = u0.reshape(bsz, t, r)
            scan_args = (u0, gate.reshape(bsz, t, r), lru_buf0[li], lru_h0[li].reshape(bsz, 1, r),
                         w["lru_conv_w"][li], _row(w["lru_conv_b"][li]), w["lru_w_a"][li], _row(w["lru_b_a"][li]),
                         w["lru_w_x"][li], _row(w["lru_b_x"][li]), _row(w["lru_lambda"][li]))
            y, ht = _lru_scan_short(*scan_args) if t == SUBLANES else _lru_scan(*scan_args, LRU_TC)
            new_h.append(ht.reshape(bsz, r))
            new_buf.append(jnp.concatenate([lru_buf0[li], u0], axis=1)[:, t:])
            x = _proj_res(y.reshape(bsz * t, r), w["lru_w_out"][li], None, x, gpost, OUT_TM, "lru_out")
            li += 1
        elif kind == 1:
            b1 = _row(w["conv_b_pw1"])
            (u,) = _norm_proj(x, gpre, [(w["conv_w_pw1"], 0), (w["conv_w_pw1"], d)], [(b1, 0), (b1, d)], _glu,
                              [F32], d, PROJ_TM, PROJ_TN, "conformer_pw1")
            u = u.reshape(bsz, t, d)
            v = _cconv(u, conv_buf0, w["conv_w_dw"], _row(w["conv_b_dw"]), _row(w["conv_ln_g"]),
                       _row(w["conv_ln_b"]), CCONV_TC)
            new_conv = jnp.concatenate([conv_buf0, u], axis=1)[:, t:]
            x = _proj_res(v.reshape(bsz * t, d), w["conv_w_pw2"], _row(w["conv_b_pw2"]), x, gpost, OUT_TM,
                          "conformer_pw2")
        else:
            q_dtype = BF16 if past is None else F32
            qs, k, v, kb, vb = _norm_proj(
                x, gpre, [(w["attn_w_q"], 0), (w["attn_w_k"], 0), (w["attn_w_v"], 0)], [], _qkv,
                [q_dtype, F32, F32, BF16, BF16], d, PROJ_TM, PROJ_TN, "attn_qkv")
            lams = [_row(w[n]) for n in ("attn_lambda_q1", "attn_lambda_k1", "attn_lambda_q2", "attn_lambda_k2")]
            g = _row(w["attn_subln_g"])
            new_k = k.reshape(bsz, t, ATTN_HEADS, 2 * HEAD_DIM)
            new_v = v.reshape(bsz, t, ATTN_HEADS, 2 * HEAD_DIM)
            if past is None:
                o = _flash_diff_attn(qs.reshape(bsz, t, d), kb.reshape(bsz, t, d), vb.reshape(bsz, t, d),
                                     lams, g, FLASH_TQ)
            else:
                cache_k, cache_v, page_table = past
                o = _paged_diff_attn(qs.reshape(bsz, t, d), new_k, new_v, cache_k, cache_v, page_table, lams, g,
                                     PAGES_PER_STEP)
            x = _proj_res(o.reshape(bsz * t, d), w["attn_w_o"], None, x, gpost, OUT_TM, "attn_out")
        x = ffn(x, layer, 1)
    return x.reshape(bsz, t, d), new_k, new_v, jnp.stack(new_h), jnp.stack(new_buf), new_conv


def kernel(x_prompt, x_sample, cache_k, cache_v, page_table, state_lru_h, state_lru_conv, state_conv, norm_pre, norm_post, ffn_w_gate, ffn_w_up, ffn_w_down, lru_w_gate, lru_w_in, lru_conv_w, lru_conv_b, lru_w_a, lru_b_a, lru_w_x, lru_b_x, lru_lambda, lru_w_out, conv_w_pw1, conv_b_pw1, conv_w_dw, conv_b_dw, conv_ln_g, conv_ln_b, conv_w_pw2, conv_b_pw2, attn_w_q, attn_w_k, attn_w_v, attn_w_o, attn_lambda_q1, attn_lambda_k1, attn_lambda_q2, attn_lambda_k2, attn_subln_g):
    w = dict(
        norm_pre=norm_pre, norm_post=norm_post,
        ffn_w_gate=ffn_w_gate, ffn_w_up=ffn_w_up, ffn_w_down=ffn_w_down,
        lru_w_gate=lru_w_gate.astype(BF16), lru_w_in=lru_w_in.astype(BF16),
        lru_conv_w=lru_conv_w, lru_conv_b=lru_conv_b,
        lru_w_a=lru_w_a.astype(BF16), lru_b_a=lru_b_a, lru_w_x=lru_w_x.astype(BF16), lru_b_x=lru_b_x,
        lru_lambda=lru_lambda, lru_w_out=lru_w_out.astype(BF16),
        conv_w_pw1=conv_w_pw1.astype(BF16), conv_b_pw1=conv_b_pw1, conv_w_dw=conv_w_dw, conv_b_dw=conv_b_dw,
        conv_ln_g=conv_ln_g, conv_ln_b=conv_ln_b, conv_w_pw2=conv_w_pw2.astype(BF16), conv_b_pw2=conv_b_pw2,
        attn_w_q=attn_w_q.astype(BF16), attn_w_k=attn_w_k.astype(BF16), attn_w_v=attn_w_v.astype(BF16),
        attn_w_o=attn_w_o.astype(BF16),
        attn_lambda_q1=attn_lambda_q1, attn_lambda_k1=attn_lambda_k1,
        attn_lambda_q2=attn_lambda_q2, attn_lambda_k2=attn_lambda_k2, attn_subln_g=attn_subln_g,
    )
    bp, s, d = x_prompt.shape
    db, ds, _ = x_sample.shape
    n_lru = state_lru_h.shape[0]
    r = state_lru_h.shape[-1]

    ffn_bf16 = {}
    y_s, k_s, v_s, h_s, lbuf_s, cbuf_s = _trunk(
        x_sample.reshape(db * ds, d), db, ds, state_lru_h, state_lru_conv, state_conv,
        (cache_k, cache_v, page_table), w, ffn_bf16)

    y_p, k_p, v_p, h_p, lbuf_p, cbuf_p = _trunk(
        x_prompt.reshape(bp * s, d), bp, s,
        jnp.zeros((n_lru, bp, r), F32), jnp.zeros((n_lru, bp, LRU_CONV - 1, r), F32),
        jnp.zeros((bp, CONV_WIDTH - 1, d), F32), None, w, ffn_bf16)

    return (y_p, y_s, k_p, v_p, h_p, lbuf_p, cbuf_p, k_s, v_s, h_s, lbuf_s, cbuf_s)
```

```python
import functools
import math

import jax
import jax.numpy as jnp
from jax import lax
from jax.experimental import pallas as pl
from jax.experimental.pallas import tpu as pltpu

F32 = jnp.float32
BF16 = jnp.bfloat16

LRU_BLOCK = 256
LRU_C = 8.0
LRU_CONV = 4
CONV_WIDTH = 31
ATTN_HEADS = 8
HEAD_DIM = 128
ATTN_LAYER = 2
LAMBDA_INIT = 0.8 - 0.6 * math.exp(-0.3 * ATTN_LAYER)
NEG_INF = -1e30
PAGE_SIZE = 128

V7X_VMEM_LIMIT_BYTES = 60 * 1024 * 1024
SUBLANES = 8
LANES = 128


def _params(semantics):
    return pltpu.CompilerParams(dimension_semantics=semantics, vmem_limit_bytes=V7X_VMEM_LIMIT_BYTES)


def _rms(x, g, eps):
    return x * lax.rsqrt(jnp.mean(x * x, axis=-1, keepdims=True) + eps) * g


def _nt_dot(a, b):
    return lax.dot_general(a, b, (((1,), (1,)), ((), ())), preferred_element_type=F32)


def _lambda_full(lq1, lk1, lq2, lk2):
    s1 = jnp.sum(lq1 * lk1, axis=-1, keepdims=True)
    s2 = jnp.sum(lq2 * lk2, axis=-1, keepdims=True)
    return jnp.exp(s1) - jnp.exp(s2) + LAMBDA_INIT


def _ffn_body(x_ref, gpre_ref, gpost_ref, o_ref, h_ref, get_weights, nj):
    j = pl.program_id(1)

    @pl.when(j == 0)
    def _():
        h_ref[...] = _rms(x_ref[...], gpre_ref[...], 1e-6).astype(BF16)
        o_ref[...] = jnp.zeros(o_ref.shape, F32)

    wgu, wd = get_weights()
    tf = wd.shape[0]
    gu = jnp.dot(h_ref[...], wgu, preferred_element_type=F32)
    parts = []
    for i in range(tf // LANES):
        g = gu[:, 2 * i * LANES:(2 * i + 1) * LANES]
        u = gu[:, (2 * i + 1) * LANES:(2 * i + 2) * LANES]
        parts.append((g * jax.nn.sigmoid(g) * u).astype(BF16))
    a = jnp.concatenate(parts, axis=1)
    o_ref[...] += jnp.dot(a, wd, preferred_element_type=F32)

    @pl.when(j == nj - 1)
    def _():
        o_ref[...] = x_ref[...] + 0.5 * _rms(o_ref[...], gpost_ref[...], 1e-6)


def _ffn_kernel(x_ref, gpre_ref, wgu_ref, wd_ref, gpost_ref, o_ref, h_ref, *, nj):
    _ffn_body(x_ref, gpre_ref, gpost_ref, o_ref, h_ref, lambda: (wgu_ref[...], wd_ref[...]), nj)


def _ffn_cast_kernel(x_ref, gpre_ref, wg_ref, wu_ref, wd_ref, gpost_ref, o_ref, wgu_out, wd_out, h_ref, *, nj):
    def get_weights():
        tf = wd_ref.shape[0]
        for i in range(tf // LANES):
            cols = slice(i * LANES, (i + 1) * LANES)
            wgu_out[:, 2 * i * LANES:(2 * i + 1) * LANES] = wg_ref[:, cols].astype(BF16)
            wgu_out[:, (2 * i + 1) * LANES:(2 * i + 2) * LANES] = wu_ref[:, cols].astype(BF16)
        wd_out[...] = wd_ref[...].astype(BF16)
        return wgu_out[...], wd_out[...]

    _ffn_body(x_ref, gpre_ref, gpost_ref, o_ref, h_ref, get_weights, nj)


def _ffn(x, gpre, wgu, wd, gpost, tm):
    n, d = x.shape
    nj, _, tf2 = wgu.shape
    tf = tf2 // 2
    tm = min(tm, n)
    vec = pl.BlockSpec((1, d), lambda i, j: (0, 0))
    return pl.pallas_call(
        functools.partial(_ffn_kernel, nj=nj),
        out_shape=jax.ShapeDtypeStruct((n, d), F32),
        grid=(n // tm, nj),
        in_specs=[pl.BlockSpec((tm, d), lambda i, j: (i, 0)), vec,
                  pl.BlockSpec((None, d, tf2), lambda i, j: (j, 0, 0)),
                  pl.BlockSpec((tf, d), lambda i, j: (j, 0)), vec],
        out_specs=pl.BlockSpec((tm, d), lambda i, j: (i, 0)),
        scratch_shapes=[pltpu.VMEM((tm, d), BF16)],
        compiler_params=_params(("parallel", "arbitrary")),
        name="ffn",
    )(x, gpre, wgu, wd, gpost)


def _ffn_cast(x, gpre, wg, wu, wd, gpost, tf, index):
    n, d = x.shape
    f = wg.shape[-1]
    nj = f // tf
    vec = pl.BlockSpec((1, d), lambda j0, j: (0, 0))
    row = pl.BlockSpec((n, d), lambda j0, j: (0, 0))
    col_tile = pl.BlockSpec((None, None, d, tf), lambda j0, j: (*index, 0, j))
    return pl.pallas_call(
        functools.partial(_ffn_cast_kernel, nj=nj),
        out_shape=[jax.ShapeDtypeStruct((n, d), F32), jax.ShapeDtypeStruct((nj, d, 2 * tf), BF16),
                   jax.ShapeDtypeStruct((f, d), BF16)],
        grid=(1, nj),
        in_specs=[row, vec, col_tile, col_tile,
                  pl.BlockSpec((None, None, tf, d), lambda j0, j: (*index, j, 0)), vec],
        out_specs=[row, pl.BlockSpec((None, d, 2 * tf), lambda j0, j: (j, 0, 0)),
                   pl.BlockSpec((tf, d), lambda j0, j: (j, 0))],
        scratch_shapes=[pltpu.VMEM((n, d), BF16)],
        compiler_params=_params(("arbitrary", "arbitrary")),
        name="ffn_cast",
    )(x, gpre, wg, wu, wd, gpost)


def _norm_proj_kernel(*refs, nw, nb, epilogue):
    x_ref, g_ref = refs[0], refs[1]
    w_refs = refs[2:2 + nw]
    b_refs = refs[2 + nw:2 + nw + nb]
    o_refs = refs[2 + nw + nb:-1]
    h_ref = refs[-1]

    @pl.when(pl.program_id(1) == 0)
    def _():
        h_ref[...] = _rms(x_ref[...], g_ref[...], 1e-6).astype(BF16)

    h = h_ref[...]
    accs = [jnp.dot(h, w[...], preferred_element_type=F32) for w in w_refs]
    if nb:
        accs = [a + b[...] for a, b in zip(accs, b_refs)]
    for o_ref, o in zip(o_refs, epilogue(accs)):
        o_ref[...] = o.astype(o_ref.dtype)


def _norm_proj(x, gpre, ws, bs, epilogue, out_dtypes, m, tm, tn, name):
    n, d = x.shape
    tm = min(tm, n)
    w_specs = [pl.BlockSpec((d, tn), functools.partial(lambda i, j, o: (0, j + o), o=off // tn)) for _, off in ws]
    b_specs = [pl.BlockSpec((1, tn), functools.partial(lambda i, j, o: (0, j + o), o=off // tn)) for _, off in bs]
    return pl.pallas_call(
        functools.partial(_norm_proj_kernel, nw=len(ws), nb=len(bs), epilogue=epilogue),
        out_shape=[jax.ShapeDtypeStruct((n, m), dt) for dt in out_dtypes],
        grid=(n // tm, m // tn),
        in_specs=[pl.BlockSpec((tm, d), lambda i, j: (i, 0)), pl.BlockSpec((1, d), lambda i, j: (0, 0))]
        + w_specs + b_specs,
        out_specs=[pl.BlockSpec((tm, tn), lambda i, j: (i, j)) for _ in out_dtypes],
        scratch_shapes=[pltpu.VMEM((tm, d), BF16)],
        compiler_params=_params(("parallel", "arbitrary")),
        name=name,
    )(x, gpre, *[w for w, _ in ws], *[b for b, _ in bs])


def _proj_res_kernel(*refs, has_bias):
    if has_bias:
        a_ref, w_ref, b_ref, x_ref, g_ref, o_ref = refs
    else:
        a_ref, w_ref, x_ref, g_ref, o_ref = refs
    y = jnp.dot(a_ref[...].astype(BF16), w_ref[...], preferred_element_type=F32)
    if has_bias:
        y = y + b_ref[...]
    o_ref[...] = x_ref[...] + _rms(y, g_ref[...], 1e-6)


def _proj_res(a, w, b, x, gpost, tm, name):
    n, k = a.shape
    d = w.shape[1]
    tm = min(tm, n)
    has_bias = b is not None
    in_specs = [pl.BlockSpec((tm, k), lambda i: (i, 0)), pl.BlockSpec((k, d), lambda i: (0, 0))]
    args = [a, w]
    if has_bias:
        in_specs.append(pl.BlockSpec((1, d), lambda i: (0, 0)))
        args.append(b)
    in_specs += [pl.BlockSpec((tm, d), lambda i: (i, 0)), pl.BlockSpec((1, d), lambda i: (0, 0))]
    args += [x, gpost]
    return pl.pallas_call(
        functools.partial(_proj_res_kernel, has_bias=has_bias),
        out_shape=jax.ShapeDtypeStruct((n, d), F32),
        grid=(n // tm,),
        in_specs=in_specs,
        out_specs=pl.BlockSpec((tm, d), lambda i: (i, 0)),
        compiler_params=_params(("parallel",)),
        name=name,
    )(*args)


def _lru_kernel(u0_ref, gate_ref, buf0_ref, h0_ref, cw_ref, cb_ref, wa_ref, ba_ref, wx_ref, bx_ref, lam_ref,
                y_ref, ht_ref, xs_ref, h_ref, *, tc, nt):
    t = pl.program_id(2)
    halo = LRU_CONV - 1

    @pl.when(t == 0)
    def _():
        xs_ref[SUBLANES - halo:SUBLANES, :] = buf0_ref[...]
        h_ref[...] = h0_ref[...]

    x = u0_ref[...]
    xs_ref[SUBLANES:SUBLANES + tc, :] = x
    cw = cw_ref[...]
    u = cw[halo:halo + 1] * x + cb_ref[...]
    for s in range(1, LRU_CONV):
        u = u + cw[halo - s:halo - s + 1] * xs_ref[SUBLANES - s:SUBLANES - s + tc, :]
    xs_ref[0:SUBLANES, :] = xs_ref[tc:tc + SUBLANES, :]

    ub = u.astype(BF16)
    r = jax.nn.sigmoid(jnp.dot(ub, wa_ref[...], preferred_element_type=F32) + ba_ref[...])
    i = jax.nn.sigmoid(jnp.dot(ub, wx_ref[...], preferred_element_type=F32) + bx_ref[...])
    lam = lam_ref[...]
    softplus_neg_lam = jnp.maximum(-lam, 0.0) + jnp.log1p(jnp.exp(-jnp.abs(lam)))
    log_a = -LRU_C * r * softplus_neg_lam
    a = jnp.exp(log_a)
    th = jnp.tanh(log_a)
    b = jnp.sqrt(-2.0 * th / (1.0 - th)) * i * u

    row = lax.broadcasted_iota(jnp.int32, (tc, LRU_BLOCK), 0)
    s = 1
    while s < tc:
        a_sh = jnp.where(row >= s, pltpu.roll(a, s, axis=0), 1.0)
        b_sh = jnp.where(row >= s, pltpu.roll(b, s, axis=0), 0.0)
        b = a * b_sh + b
        a = a * a_sh
        s *= 2
    hs = a * h_ref[...] + b
    h_ref[...] = hs[tc - 1:tc, :]
    y_ref[...] = (hs * gate_ref[...]).astype(y_ref.dtype)

    @pl.when(t == nt - 1)
    def _():
        ht_ref[...] = hs[tc - 1:tc, :]


def _lru_scan(u0, gate, buf0, h0, cw, cb, wa, ba, wx, bx, lam, tc):
    bsz, t, r = u0.shape
    tc = min(tc, t)
    nt = t // tc
    nb = r // LRU_BLOCK
    vec = pl.BlockSpec((1, LRU_BLOCK), lambda b, n, k: (0, n))
    return pl.pallas_call(
        functools.partial(_lru_kernel, tc=tc, nt=nt),
        out_shape=[jax.ShapeDtypeStruct((bsz, t, r), BF16), jax.ShapeDtypeStruct((bsz, 1, r), F32)],
        grid=(bsz, nb, nt),
        in_specs=[
            pl.BlockSpec((None, tc, LRU_BLOCK), lambda b, n, k: (b, k, n)),
            pl.BlockSpec((None, tc, LRU_BLOCK), lambda b, n, k: (b, k, n)),
            pl.BlockSpec((None, LRU_CONV - 1, LRU_BLOCK), lambda b, n, k: (b, 0, n)),
            pl.BlockSpec((None, 1, LRU_BLOCK), lambda b, n, k: (b, 0, n)),
            pl.BlockSpec((LRU_CONV, LRU_BLOCK), lambda b, n, k: (0, n)),
            vec,
            pl.BlockSpec((None, LRU_BLOCK, LRU_BLOCK), lambda b, n, k: (n, 0, 0)),
            vec,
            pl.BlockSpec((None, LRU_BLOCK, LRU_BLOCK), lambda b, n, k: (n, 0, 0)),
            vec,
            vec,
        ],
        out_specs=[
            pl.BlockSpec((None, tc, LRU_BLOCK), lambda b, n, k: (b, k, n)),
            pl.BlockSpec((None, 1, LRU_BLOCK), lambda b, n, k: (b, 0, n)),
        ],
        scratch_shapes=[pltpu.VMEM((tc + SUBLANES, LRU_BLOCK), F32), pltpu.VMEM((1, LRU_BLOCK), F32)],
        compiler_params=_params(("parallel", "parallel", "arbitrary")),
        name="lru_scan",
    )(u0, gate, buf0, h0, cw, cb, wa, ba, wx, bx, lam)


def _lru_short_kernel(u0_ref, gate_ref, past_ref, h0_ref, cw_ref, cb_ref, wa_ref, ba_ref, wx_ref, bx_ref, lam_ref,
                      y_ref, ht_ref):
    x = u0_ref[...]
    bsz, t, _ = x.shape
    halo = LRU_CONV - 1
    pos = lax.broadcasted_iota(jnp.int32, x.shape, 1)
    past = past_ref[...]
    cw = cw_ref[...]
    u = cw[halo:halo + 1] * x + cb_ref[...]
    for s in range(1, LRU_CONV):
        shifted = jnp.where(pos >= s, pltpu.roll(x, s, axis=1), pltpu.roll(past, s, axis=1))
        u = u + cw[halo - s:halo - s + 1] * shifted

    u2 = u.reshape(bsz * t, LRU_BLOCK)
    ub = u2.astype(BF16)
    r = jax.nn.sigmoid(jnp.dot(ub, wa_ref[...], preferred_element_type=F32) + ba_ref[...])
    i = jax.nn.sigmoid(jnp.dot(ub, wx_ref[...], preferred_element_type=F32) + bx_ref[...])
    lam = lam_ref[...]
    softplus_neg_lam = jnp.maximum(-lam, 0.0) + jnp.log1p(jnp.exp(-jnp.abs(lam)))
    log_a = -LRU_C * r * softplus_neg_lam
    th = jnp.tanh(log_a)
    a = jnp.exp(log_a).reshape(x.shape)
    b = (jnp.sqrt(-2.0 * th / (1.0 - th)) * i * u2).reshape(x.shape)

    s = 1
    while s < t:
        a_sh = jnp.where(pos >= s, pltpu.roll(a, s, axis=1), 1.0)
        b_sh = jnp.where(pos >= s, pltpu.roll(b, s, axis=1), 0.0)
        b = a * b_sh + b
        a = a * a_sh
        s *= 2
    hs = a * h0_ref[...] + b
    y_ref[...] = (hs * gate_ref[...]).astype(y_ref.dtype)
    ht_ref[...] = hs[:, t - 1:t, :]


def _lru_scan_short(u0, gate, buf0, h0, cw, cb, wa, ba, wx, bx, lam):
    bsz, t, r = u0.shape
    past = jnp.concatenate([jnp.zeros((bsz, t - (LRU_CONV - 1), r), F32), buf0], axis=1)
    seq = pl.BlockSpec((bsz, t, LRU_BLOCK), lambda n: (0, 0, n))
    vec = pl.BlockSpec((1, LRU_BLOCK), lambda n: (0, n))
    mat = pl.BlockSpec((None, LRU_BLOCK, LRU_BLOCK), lambda n: (n, 0, 0))
    state = pl.BlockSpec((bsz, 1, LRU_BLOCK), lambda n: (0, 0, n))
    return pl.pallas_call(
        _lru_short_kernel,
        out_shape=[jax.ShapeDtypeStruct((bsz, t, r), BF16), jax.ShapeDtypeStruct((bsz, 1, r), F32)],
        grid=(r // LRU_BLOCK,),
        in_specs=[seq, seq, seq, state, pl.BlockSpec((LRU_CONV, LRU_BLOCK), lambda n: (0, n)), vec,
                  mat, vec, mat, vec, vec],
        out_specs=[seq, state],
        compiler_params=_params(("parallel",)),
        name="lru_scan_short",
    )(u0, gate, past, h0, cw, cb, wa, ba, wx, bx, lam)


CONV_HALO = 32
CONV_LANES = 256


def _cconv_kernel(u_ref, buf0_ref, w_ref, b_ref, lng_ref, lnb_ref, o_ref, xs_ref, acc_ref, *, tc):
    t = pl.program_id(1)
    d = u_ref.shape[-1]
    pad = CONV_HALO - (CONV_WIDTH - 1)

    @pl.when(t == 0)
    def _():
        xs_ref[pad:CONV_HALO, :] = buf0_ref[...]

    xs_ref[CONV_HALO:CONV_HALO + tc, :] = u_ref[...]
    for c in range(d // CONV_LANES):
        cs = slice(c * CONV_LANES, (c + 1) * CONV_LANES)
        acc = b_ref[:, cs]
        for r in range(SUBLANES):
            span = tc + SUBLANES * ((CONV_WIDTH - 1 - r) // SUBLANES)
            z = xs_ref[pad + r:pad + r + span, cs]
            part = None
            for k in range(r, CONV_WIDTH, SUBLANES):
                term = w_ref[k:k + 1, cs] * z[k - r:k - r + tc]
                part = term if part is None else part + term
            acc = acc + part
        acc_ref[:, cs] = acc
    xs_ref[0:CONV_HALO, :] = xs_ref[tc:tc + CONV_HALO, :]

    v = acc_ref[...]
    mu = jnp.mean(v, axis=-1, keepdims=True)
    vc = v - mu
    var = jnp.mean(vc * vc, axis=-1, keepdims=True)
    y = vc * lax.rsqrt(var + 1e-5) * lng_ref[...] + lnb_ref[...]
    o_ref[...] = (y * jax.nn.sigmoid(y)).astype(o_ref.dtype)


def _cconv(u, buf0, w, b, lng, lnb, tc):
    bsz, t, d = u.shape
    tc = min(tc, t)
    vec = pl.BlockSpec((1, d), lambda bb, k: (0, 0))
    return pl.pallas_call(
        functools.partial(_cconv_kernel, tc=tc),
        out_shape=jax.ShapeDtypeStruct((bsz, t, d), BF16),
        grid=(bsz, t // tc),
        in_specs=[
            pl.BlockSpec((None, tc, d), lambda bb, k: (bb, k, 0)),
            pl.BlockSpec((None, CONV_WIDTH - 1, d), lambda bb, k: (bb, 0, 0)),
            pl.BlockSpec((CONV_WIDTH, d), lambda bb, k: (0, 0)),
            vec, vec, vec,
        ],
        out_specs=pl.BlockSpec((None, tc, d), lambda bb, k: (bb, k, 0)),
        scratch_shapes=[pltpu.VMEM((tc + CONV_HALO, d), F32), pltpu.VMEM((tc, d), F32)],
        compiler_params=_params(("parallel", "arbitrary")),
        name="conformer_conv",
    )(u, buf0, w, b, lng, lnb)


QK_LOG2_SCALE = HEAD_DIM ** -0.5 * math.log2(math.e)


def _diff_finish(o1, o2, lam, g):
    o = o1 - lam * o2
    return _rms(o, g, 1e-5) * (1.0 - LAMBDA_INIT)


def _flash_kernel(qi_ref, ki_ref, lq1_ref, lk1_ref, lq2_ref, lk2_ref, g_ref, q_ref, k_ref, v_ref, o_ref,
                  m_ref, l_ref, acc_ref, *, tq, sb):
    step = pl.program_id(2)
    qi = qi_ref[step]
    ki = ki_ref[step]

    @pl.when(ki == 0)
    def _():
        m_ref[...] = jnp.full(m_ref.shape, NEG_INF, F32)
        l_ref[...] = jnp.zeros(l_ref.shape, F32)
        acc_ref[...] = jnp.zeros(acc_ref.shape, F32)

    def update(rows, keys, masked):
        nr = rows.stop - rows.start
        nk = keys.stop - keys.start
        if masked:
            keep = (lax.broadcasted_iota(jnp.int32, (nr, nk), 0) >= lax.broadcasted_iota(jnp.int32, (nr, nk), 1))
        v = v_ref[keys, :]
        for c in range(2):
            cs = slice(c * HEAD_DIM, (c + 1) * HEAD_DIM)
            s = _nt_dot(q_ref[rows, cs], k_ref[keys, cs])
            if masked:
                s = jnp.where(keep, s, NEG_INF)
            m_prev = m_ref[c, rows]
            m_new = jnp.maximum(m_prev, jnp.max(s, axis=-1, keepdims=True))
            alpha = jnp.exp2(m_prev - m_new)
            ps = [jnp.exp2(s[:, j * LANES:(j + 1) * LANES] - m_new) for j in range(nk // LANES)]
            l_new = alpha * l_ref[c, rows]
            for p in ps:
                l_new = l_new + p
            l_ref[c, rows] = l_new
            m_ref[c, rows] = m_new
            p = jnp.concatenate(ps, axis=1).astype(BF16)
            acc_ref[c, rows] = jnp.concatenate([alpha, alpha], axis=1) * acc_ref[c, rows] + jnp.dot(
                p, v, preferred_element_type=F32)

    nsub = tq // sb
    sub = [slice(i * sb, (i + 1) * sb) for i in range(nsub)]

    @pl.when(ki < qi)
    def _():
        for keys in sub:
            update(slice(0, tq), keys, False)

    @pl.when(ki == qi)
    def _():
        for i in range(nsub):
            for j in range(i + 1):
                update(sub[i], sub[j], i == j)
        lam = _lambda_full(lq1_ref[...], lk1_ref[...], lq2_ref[...], lk2_ref[...])
        o1 = acc_ref[0] / jnp.sum(l_ref[0], axis=-1, keepdims=True)
        o2 = acc_ref[1] / jnp.sum(l_ref[1], axis=-1, keepdims=True)
        o_ref[...] = _diff_finish(o1, o2, lam, g_ref[...]).astype(o_ref.dtype)


def _flash_diff_attn(q, k, v, lams, g, tq, sb):
    bsz, s, _ = q.shape
    hd2 = 2 * HEAD_DIM
    nq = s // tq
    pairs = [(i, j) for i in range(nq) for j in range(i + 1)]
    qi_tab = jnp.asarray([p[0] for p in pairs], jnp.int32)
    ki_tab = jnp.asarray([p[1] for p in pairs], jnp.int32)
    small = pl.BlockSpec((1, HEAD_DIM), lambda b, h, t, qt, kt: (0, 0))
    q_spec = pl.BlockSpec((None, tq, hd2), lambda b, h, t, qt, kt: (b, qt[t], h))
    kv_spec = pl.BlockSpec((None, tq, hd2), lambda b, h, t, qt, kt: (b, kt[t], h))
    grid_spec = pltpu.PrefetchScalarGridSpec(
        num_scalar_prefetch=2,
        grid=(bsz, ATTN_HEADS, len(pairs)),
        in_specs=[small, small, small, small, pl.BlockSpec((1, hd2), lambda b, h, t, qt, kt: (0, 0)),
                  q_spec, kv_spec, kv_spec],
        out_specs=q_spec,
        scratch_shapes=[pltpu.VMEM((2, tq, LANES), F32), pltpu.VMEM((2, tq, LANES), F32),
                        pltpu.VMEM((2, tq, hd2), F32)],
    )
    return pl.pallas_call(
        functools.partial(_flash_kernel, tq=tq, sb=sb),
        out_shape=jax.ShapeDtypeStruct(q.shape, BF16),
        grid_spec=grid_spec,
        compiler_params=_params(("parallel", "parallel", "arbitrary")),
        name="flash_diff_attn",
    )(qi_tab, ki_tab, *lams, g, q, k, v)


def _paged_kernel(pt_ref, lq1_ref, lk1_ref, lq2_ref, lk2_ref, g_ref, q_ref, kn_ref, vn_ref, *rest, pp, nj):
    del pt_ref
    k_refs = rest[:pp]
    v_refs = rest[pp:2 * pp]
    o_ref = rest[2 * pp]
    wq_ref, bias_ref, m_ref, l_ref, acc_ref = rest[2 * pp + 1:]
    j = pl.program_id(1)
    t = q_ref.shape[0]
    rows = 2 * ATTN_HEADS * t
    hd2 = 2 * HEAD_DIM
    nkeys = PAGE_SIZE * ATTN_HEADS

    @pl.when(j == 0)
    def _():
        q = q_ref[...]
        lane = lax.broadcasted_iota(jnp.int32, (t, hd2), 1)
        pieces = []
        for h in range(ATTN_HEADS):
            qh = q[:, h * hd2:(h + 1) * hd2]
            pieces.append(jnp.where(lane < HEAD_DIM, qh, 0.0))
            pieces.append(jnp.where(lane >= HEAD_DIM, qh, 0.0))
        wq_ref[...] = jnp.concatenate(pieces, axis=0).astype(BF16)
        row_head = lax.broadcasted_iota(jnp.int32, (rows, nkeys), 0) // (2 * t)
        key_head = lax.broadcasted_iota(jnp.int32, (rows, nkeys), 1) % ATTN_HEADS
        bias_ref[...] = jnp.where(row_head == key_head, 0.0, NEG_INF)
        m_ref[...] = jnp.full(m_ref.shape, NEG_INF, F32)
        l_ref[...] = jnp.zeros(l_ref.shape, F32)
        acc_ref[...] = jnp.zeros(acc_ref.shape, F32)

    def update(k2s, v2s, biases):
        wq = wq_ref[...]
        ss = [_nt_dot(wq, k2.astype(BF16)) + bias for k2, bias in zip(k2s, biases)]
        m_prev = m_ref[...]
        m_new = m_prev
        for s in ss:
            m_new = jnp.maximum(m_new, jnp.max(s, axis=-1, keepdims=True))
        alpha = jnp.exp2(m_prev - m_new)
        ps = [jnp.exp2(s - m_new) for s in ss]
        l_new = alpha * l_ref[...]
        pv = None
        for p, v2 in zip(ps, v2s):
            l_new = l_new + jnp.sum(p, axis=-1, keepdims=True)
            d = jnp.dot(p.astype(BF16), v2.astype(BF16), preferred_element_type=F32)
            pv = d if pv is None else pv + d
        l_ref[...] = l_new
        m_ref[...] = m_new
        acc_ref[...] = alpha * acc_ref[...] + pv

    update([kr[...].reshape(nkeys, hd2) for kr in k_refs], [vr[...].reshape(nkeys, hd2) for vr in v_refs],
           [bias_ref[...]] * pp)

    @pl.when(j == nj - 1)
    def _():
        n_new = t * ATTN_HEADS
        row = lax.broadcasted_iota(jnp.int32, (rows, n_new), 0)
        col = lax.broadcasted_iota(jnp.int32, (rows, n_new), 1)
        keep = (row // (2 * t) == col % ATTN_HEADS) & (col // ATTN_HEADS <= row % t)
        update([kn_ref[...].reshape(n_new, hd2)], [vn_ref[...].reshape(n_new, hd2)],
               [jnp.where(keep, 0.0, NEG_INF)])
        lam = _lambda_full(lq1_ref[...], lk1_ref[...], lq2_ref[...], lk2_ref[...])
        o = acc_ref[...] / l_ref[...]
        for h in range(ATTN_HEADS):
            o1 = o[h * 2 * t:h * 2 * t + t]
            o2 = o[h * 2 * t + t:(h + 1) * 2 * t]
            o_ref[:, h * hd2:(h + 1) * hd2] = _diff_finish(o1, o2, lam, g_ref[...]).astype(o_ref.dtype)


def _paged_diff_attn(q, k_new, v_new, cache_k, cache_v, page_table, lams, g, pp):
    bsz, t, dk = q.shape
    hd2 = 2 * HEAD_DIM
    n_pages = page_table.shape[1]
    nj = n_pages // pp
    rows = 2 * ATTN_HEADS * t
    small = pl.BlockSpec((1, HEAD_DIM), lambda b, j, pt: (0, 0))
    tok = pl.BlockSpec((None, t, ATTN_HEADS, hd2), lambda b, j, pt: (b, 0, 0, 0))
    page_specs = [
        pl.BlockSpec((None, PAGE_SIZE, ATTN_HEADS, hd2),
                     functools.partial(lambda b, j, pt, r: (pt[b, j * pp + r], 0, 0, 0), r=r))
        for r in range(pp)
    ]
    grid_spec = pltpu.PrefetchScalarGridSpec(
        num_scalar_prefetch=1,
        grid=(bsz, nj),
        in_specs=[small, small, small, small, pl.BlockSpec((1, hd2), lambda b, j, pt: (0, 0)),
                  pl.BlockSpec((None, t, dk), lambda b, j, pt: (b, 0, 0)), tok, tok] + page_specs + page_specs,
        out_specs=pl.BlockSpec((None, t, dk), lambda b, j, pt: (b, 0, 0)),
        scratch_shapes=[pltpu.VMEM((rows, hd2), BF16), pltpu.VMEM((rows, PAGE_SIZE * ATTN_HEADS), F32),
                        pltpu.VMEM((rows, 1), F32), pltpu.VMEM((rows, 1), F32), pltpu.VMEM((rows, hd2), F32)],
    )
    return pl.pallas_call(
        functools.partial(_paged_kernel, pp=pp, nj=nj),
        out_shape=jax.ShapeDtypeStruct(q.shape, BF16),
        grid_spec=grid_spec,
        compiler_params=_params(("parallel", "arbitrary")),
        name="paged_diff_attn",
    )(page_table, *lams, g, q, k_new, v_new, *([cache_k] * pp), *([cache_v] * pp))


FFN_TM = 512
FFN_TF = 512
PROJ_TM = 1024
PROJ_TN = 512
OUT_TM = 512
LRU_TC = 256
CCONV_TC = 128
FLASH_TQ = 2048
FLASH_SUB = 512
PAGES_PER_STEP = 8


def _gelu_gate(accs):
    return [jax.nn.gelu(accs[0], approximate=True), accs[1]]


def _glu(accs):
    return [accs[0] * jax.nn.sigmoid(accs[1])]


def _qkv(accs):
    q, k, v = accs
    return [q * QK_LOG2_SCALE, k, v, k, v]


def _row(v):
    return v.reshape(1, -1)


def _trunk(x, bsz, t, lru_h0, lru_buf0, conv_buf0, past, w, ffn_bf16):
    d = x.shape[-1]
    new_h, new_buf = [], []
    new_conv = new_k = new_v = None
    li = 0

    def ffn(x, layer, half):
        gpre, gpost = _row(w["norm_pre"][layer, 2 * half]), _row(w["norm_post"][layer, 2 * half])
        if (layer, half) in ffn_bf16:
            return _ffn(x, gpre, *ffn_bf16[layer, half], gpost, FFN_TM)
        x, *ffn_bf16[layer, half] = _ffn_cast(x, gpre, w["ffn_w_gate"], w["ffn_w_up"], w["ffn_w_down"], gpost,
                                              FFN_TF, (layer, half))
        return x

    for layer in range(4):
        x = ffn(x, layer, 0)
        gpre = _row(w["norm_pre"][layer, 1])
        gpost = _row(w["norm_post"][layer, 1])
        kind = layer % 3
        if kind == 0:
            r = w["lru_w_gate"].shape[-1]
            gate, u0 = _norm_proj(x, gpre, [(w["lru_w_gate"][li], 0), (w["lru_w_in"][li], 0)], [], _gelu_gate,
                                  [F32, F32], r, PROJ_TM, PROJ_TN, "lru_in")
            u0 = u0.reshape(bsz, t, r)
            scan_args = (u0, gate.reshape(bsz, t, r), lru_buf0[li], lru_h0[li].reshape(bsz, 1, r),
                         w["lru_conv_w"][li], _row(w["lru_conv_b"][li]), w["lru_w_a"][li], _row(w["lru_b_a"][li]),
                         w["lru_w_x"][li], _row(w["lru_b_x"][li]), _row(w["lru_lambda"][li]))
            y, ht = _lru_scan_short(*scan_args) if t == SUBLANES else _lru_scan(*scan_args, LRU_TC)
            new_h.append(ht.reshape(bsz, r))
            new_buf.append(jnp.concatenate([lru_buf0[li], u0], axis=1)[:, t:])
            x = _proj_res(y.reshape(bsz * t, r), w["lru_w_out"][li], None, x, gpost, OUT_TM, "lru_out")
            li += 1
        elif kind == 1:
            b1 = _row(w["conv_b_pw1"])
            (u,) = _norm_proj(x, gpre, [(w["conv_w_pw1"], 0), (w["conv_w_pw1"], d)], [(b1, 0), (b1, d)], _glu,
                              [F32], d, PROJ_TM, PROJ_TN, "conformer_pw1")
            u = u.reshape(bsz, t, d)
            v = _cconv(u, conv_buf0, w["conv_w_dw"], _row(w["conv_b_dw"]), _row(w["conv_ln_g"]),
                       _row(w["conv_ln_b"]), CCONV_TC)
            new_conv = jnp.concatenate([conv_buf0, u], axis=1)[:, t:]
            x = _proj_res(v.reshape(bsz * t, d), w["conv_w_pw2"], _row(w["conv_b_pw2"]), x, gpost, OUT_TM,
                          "conformer_pw2")
        else:
            q_dtype = BF16 if past is None else F32
            qs, k, v, kb, vb = _norm_proj(
                x, gpre, [(w["attn_w_q"], 0), (w["attn_w_k"], 0), (w["attn_w_v"], 0)], [], _qkv,
                [q_dtype, F32, F32, BF16, BF16], d, PROJ_TM, PROJ_TN, "attn_qkv")
            lams = [_row(w[n]) for n in ("attn_lambda_q1", "attn_lambda_k1", "attn_lambda_q2", "attn_lambda_k2")]
            g = _row(w["attn_subln_g"])
            new_k = k.reshape(bsz, t, ATTN_HEADS, 2 * HEAD_DIM)
            new_v = v.reshape(bsz, t, ATTN_HEADS, 2 * HEAD_DIM)
            if past is None:
                o = _flash_diff_attn(qs.reshape(bsz, t, d), kb.reshape(bsz, t, d), vb.reshape(bsz, t, d),
                                     lams, g, FLASH_TQ, FLASH_SUB)
            else:
                cache_k, cache_v, page_table = past
                o = _paged_diff_attn(qs.reshape(bsz, t, d), new_k, new_v, cache_k, cache_v, page_table, lams, g,
                                     PAGES_PER_STEP)
            x = _proj_res(o.reshape(bsz * t, d), w["attn_w_o"], None, x, gpost, OUT_TM, "attn_out")
        x = ffn(x, layer, 1)
    return x.reshape(bsz, t, d), new_k, new_v, jnp.stack(new_h), jnp.stack(new_buf), new_conv


def kernel(x_prompt, x_sample, cache_k, cache_v, page_table, state_lru_h, state_lru_conv, state_conv, norm_pre, norm_post, ffn_w_gate, ffn_w_up, ffn_w_down, lru_w_gate, lru_w_in, lru_conv_w, lru_conv_b, lru_w_a, lru_b_a, lru_w_x, lru_b_x, lru_lambda, lru_w_out, conv_w_pw1, conv_b_pw1, conv_w_dw, conv_b_dw, conv_ln_g, conv_ln_b, conv_w_pw2, conv_b_pw2, attn_w_q, attn_w_k, attn_w_v, attn_w_o, attn_lambda_q1, attn_lambda_k1, attn_lambda_q2, attn_lambda_k2, attn_subln_g):
    w = dict(
        norm_pre=norm_pre, norm_post=norm_post,
        ffn_w_gate=ffn_w_gate, ffn_w_up=ffn_w_up, ffn_w_down=ffn_w_down,
        lru_w_gate=lru_w_gate.astype(BF16), lru_w_in=lru_w_in.astype(BF16),
        lru_conv_w=lru_conv_w, lru_conv_b=lru_conv_b,
        lru_w_a=lru_w_a.astype(BF16), lru_b_a=lru_b_a, lru_w_x=lru_w_x.astype(BF16), lru_b_x=lru_b_x,
        lru_lambda=lru_lambda, lru_w_out=lru_w_out.astype(BF16),
        conv_w_pw1=conv_w_pw1.astype(BF16), conv_b_pw1=conv_b_pw1, conv_w_dw=conv_w_dw, conv_b_dw=conv_b_dw,
        conv_ln_g=conv_ln_g, conv_ln_b=conv_ln_b, conv_w_pw2=conv_w_pw2.astype(BF16), conv_b_pw2=conv_b_pw2,
        attn_w_q=attn_w_q.astype(BF16), attn_w_k=attn_w_k.astype(BF16), attn_w_v=attn_w_v.astype(BF16),
        attn_w_o=attn_w_o.astype(BF16),
        attn_lambda_q1=attn_lambda_q1, attn_lambda_k1=attn_lambda_k1,
        attn_lambda_q2=attn_lambda_q2, attn_lambda_k2=attn_lambda_k2, attn_subln_g=attn_subln_g,
    )
    bp, s, d = x_prompt.shape
    db, ds, _ = x_sample.shape
    n_lru = state_lru_h.shape[0]
    r = state_lru_h.shape[-1]

    ffn_bf16 = {}
    y_s, k_s, v_s, h_s, lbuf_s, cbuf_s = _trunk(
        x_sample.reshape(db * ds, d), db, ds, state_lru_h, state_lru_conv, state_conv,
        (cache_k, cache_v, page_table), w, ffn_bf16)

    y_p, k_p, v_p, h_p, lbuf_p, cbuf_p = _trunk(
        x_prompt.reshape(bp * s, d), bp, s,
        jnp.zeros((n_lru, bp, r), F32), jnp.zeros((n_lru, bp, LRU_CONV - 1, r), F32),
        jnp.zeros((bp, CONV_WIDTH - 1, d), F32), None, w, ffn_bf16)

    return (y_p, y_s, k_p, v_p, h_p, lbuf_p, cbuf_p, k_s, v_s, h_s, lbuf_s, cbuf_s)
```

```python
import functools
import math

import jax
import jax.numpy as jnp
from jax import lax
from jax.experimental import pallas as pl
from jax.experimental.pallas import tpu as pltpu

F32 = jnp.float32
BF16 = jnp.bfloat16

LRU_BLOCK = 256
LRU_C = 8.0
LRU_CONV = 4
CONV_WIDTH = 31
ATTN_HEADS = 8
HEAD_DIM = 128
ATTN_LAYER = 2
LAMBDA_INIT = 0.8 - 0.6 * math.exp(-0.3 * ATTN_LAYER)
NEG_INF = -1e30
PAGE_SIZE = 128

V7X_VMEM_LIMIT_BYTES = 60 * 1024 * 1024
SUBLANES = 8
LANES = 128


def _params(semantics):
    return pltpu.CompilerParams(dimension_semantics=semantics, vmem_limit_bytes=V7X_VMEM_LIMIT_BYTES)


def _rms(x, g, eps):
    return x * lax.rsqrt(jnp.mean(x * x, axis=-1, keepdims=True) + eps) * g


def _nt_dot(a, b):
    return lax.dot_general(a, b, (((1,), (1,)), ((), ())), preferred_element_type=F32)


def _lambda_full(lq1, lk1, lq2, lk2):
    s1 = jnp.sum(lq1 * lk1, axis=-1, keepdims=True)
    s2 = jnp.sum(lq2 * lk2, axis=-1, keepdims=True)
    return jnp.exp(s1) - jnp.exp(s2) + LAMBDA_INIT


def _ffn_body(x_ref, gpre_ref, gpost_ref, o_ref, h_ref, get_weights, nj):
    j = pl.program_id(1)

    @pl.when(j == 0)
    def _():
        h_ref[...] = _rms(x_ref[...], gpre_ref[...], 1e-6).astype(BF16)
        o_ref[...] = jnp.zeros(o_ref.shape, F32)

    wgu, wd = get_weights()
    tf = wd.shape[0]
    gu = jnp.dot(h_ref[...], wgu, preferred_element_type=F32)
    parts = []
    for i in range(tf // LANES):
        g = gu[:, 2 * i * LANES:(2 * i + 1) * LANES]
        u = gu[:, (2 * i + 1) * LANES:(2 * i + 2) * LANES]
        parts.append((g * jax.nn.sigmoid(g) * u).astype(BF16))
    a = jnp.concatenate(parts, axis=1)
    o_ref[...] += jnp.dot(a, wd, preferred_element_type=F32)

    @pl.when(j == nj - 1)
    def _():
        o_ref[...] = x_ref[...] + 0.5 * _rms(o_ref[...], gpost_ref[...], 1e-6)


def _ffn_kernel(x_ref, gpre_ref, wgu_ref, wd_ref, gpost_ref, o_ref, h_ref, *, nj):
    _ffn_body(x_ref, gpre_ref, gpost_ref, o_ref, h_ref, lambda: (wgu_ref[...], wd_ref[...]), nj)


def _ffn_cast_kernel(x_ref, gpre_ref, wg_ref, wu_ref, wd_ref, gpost_ref, o_ref, wgu_out, wd_out, h_ref, *, nj):
    def get_weights():
        tf = wd_ref.shape[0]
        for i in range(tf // LANES):
            cols = slice(i * LANES, (i + 1) * LANES)
            wgu_out[:, 2 * i * LANES:(2 * i + 1) * LANES] = wg_ref[:, cols].astype(BF16)
            wgu_out[:, (2 * i + 1) * LANES:(2 * i + 2) * LANES] = wu_ref[:, cols].astype(BF16)
        wd_out[...] = wd_ref[...].astype(BF16)
        return wgu_out[...], wd_out[...]

    _ffn_body(x_ref, gpre_ref, gpost_ref, o_ref, h_ref, get_weights, nj)


def _ffn(x, gpre, wgu, wd, gpost, tm):
    n, d = x.shape
    nj, _, tf2 = wgu.shape
    tf = tf2 // 2
    tm = min(tm, n)
    vec = pl.BlockSpec((1, d), lambda i, j: (0, 0))
    return pl.pallas_call(
        functools.partial(_ffn_kernel, nj=nj),
        out_shape=jax.ShapeDtypeStruct((n, d), F32),
        grid=(n // tm, nj),
        in_specs=[pl.BlockSpec((tm, d), lambda i, j: (i, 0), pipeline_mode=pl.Buffered(1)), vec,
                  pl.BlockSpec((None, d, tf2), lambda i, j: (j, 0, 0)),
                  pl.BlockSpec((tf, d), lambda i, j: (j, 0)), vec],
        out_specs=pl.BlockSpec((tm, d), lambda i, j: (i, 0)),
        scratch_shapes=[pltpu.VMEM((tm, d), BF16)],
        compiler_params=_params(("parallel", "arbitrary")),
        name="ffn",
    )(x, gpre, wgu, wd, gpost)


def _ffn_cast(x, gpre, wg, wu, wd, gpost, tf, index):
    n, d = x.shape
    f = wg.shape[-1]
    nj = f // tf
    vec = pl.BlockSpec((1, d), lambda j0, j: (0, 0))
    row = pl.BlockSpec((n, d), lambda j0, j: (0, 0))
    col_tile = pl.BlockSpec((None, None, d, tf), lambda j0, j: (*index, 0, j))
    return pl.pallas_call(
        functools.partial(_ffn_cast_kernel, nj=nj),
        out_shape=[jax.ShapeDtypeStruct((n, d), F32), jax.ShapeDtypeStruct((nj, d, 2 * tf), BF16),
                   jax.ShapeDtypeStruct((f, d), BF16)],
        grid=(1, nj),
        in_specs=[row, vec, col_tile, col_tile,
                  pl.BlockSpec((None, None, tf, d), lambda j0, j: (*index, j, 0)), vec],
        out_specs=[row, pl.BlockSpec((None, d, 2 * tf), lambda j0, j: (j, 0, 0)),
                   pl.BlockSpec((tf, d), lambda j0, j: (j, 0))],
        scratch_shapes=[pltpu.VMEM((n, d), BF16)],
        compiler_params=_params(("arbitrary", "arbitrary")),
        name="ffn_cast",
    )(x, gpre, wg, wu, wd, gpost)


def _norm_proj_kernel(*refs, nw, nb, epilogue):
    x_ref, g_ref = refs[0], refs[1]
    w_refs = refs[2:2 + nw]
    b_refs = refs[2 + nw:2 + nw + nb]
    o_refs = refs[2 + nw + nb:-1]
    h_ref = refs[-1]

    @pl.when(pl.program_id(1) == 0)
    def _():
        h_ref[...] = _rms(x_ref[...], g_ref[...], 1e-6).astype(BF16)

    h = h_ref[...]
    accs = [jnp.dot(h, w[...], preferred_element_type=F32) for w in w_refs]
    if nb:
        accs = [a + b[...] for a, b in zip(accs, b_refs)]
    for o_ref, o in zip(o_refs, epilogue(accs)):
        o_ref[...] = o.astype(o_ref.dtype)


def _norm_proj(x, gpre, ws, bs, epilogue, out_dtypes, m, tm, tn, name):
    n, d = x.shape
    tm = min(tm, n)
    w_specs = [pl.BlockSpec((d, tn), functools.partial(lambda i, j, o: (0, j + o), o=off // tn)) for _, off in ws]
    b_specs = [pl.BlockSpec((1, tn), functools.partial(lambda i, j, o: (0, j + o), o=off // tn)) for _, off in bs]
    return pl.pallas_call(
        functools.partial(_norm_proj_kernel, nw=len(ws), nb=len(bs), epilogue=epilogue),
        out_shape=[jax.ShapeDtypeStruct((n, m), dt) for dt in out_dtypes],
        grid=(n // tm, m // tn),
        in_specs=[pl.BlockSpec((tm, d), lambda i, j: (i, 0)), pl.BlockSpec((1, d), lambda i, j: (0, 0))]
        + w_specs + b_specs,
        out_specs=[pl.BlockSpec((tm, tn), lambda i, j: (i, j)) for _ in out_dtypes],
        scratch_shapes=[pltpu.VMEM((tm, d), BF16)],
        compiler_params=_params(("parallel", "arbitrary")),
        name=name,
    )(x, gpre, *[w for w, _ in ws], *[b for b, _ in bs])


def _proj_res_kernel(*refs, has_bias):
    if has_bias:
        a_ref, w_ref, b_ref, x_ref, g_ref, o_ref = refs
    else:
        a_ref, w_ref, x_ref, g_ref, o_ref = refs
    y = jnp.dot(a_ref[...].astype(BF16), w_ref[...], preferred_element_type=F32)
    if has_bias:
        y = y + b_ref[...]
    o_ref[...] = x_ref[...] + _rms(y, g_ref[...], 1e-6)


def _proj_res(a, w, b, x, gpost, tm, name):
    n, k = a.shape
    d = w.shape[1]
    tm = min(tm, n)
    has_bias = b is not None
    in_specs = [pl.BlockSpec((tm, k), lambda i: (i, 0)), pl.BlockSpec((k, d), lambda i: (0, 0))]
    args = [a, w]
    if has_bias:
        in_specs.append(pl.BlockSpec((1, d), lambda i: (0, 0)))
        args.append(b)
    in_specs += [pl.BlockSpec((tm, d), lambda i: (i, 0)), pl.BlockSpec((1, d), lambda i: (0, 0))]
    args += [x, gpost]
    return pl.pallas_call(
        functools.partial(_proj_res_kernel, has_bias=has_bias),
        out_shape=jax.ShapeDtypeStruct((n, d), F32),
        grid=(n // tm,),
        in_specs=in_specs,
        out_specs=pl.BlockSpec((tm, d), lambda i: (i, 0)),
        compiler_params=_params(("parallel",)),
        name=name,
    )(*args)


def _lru_kernel(u0_ref, gate_ref, buf0_ref, h0_ref, cw_ref, cb_ref, wa_ref, ba_ref, wx_ref, bx_ref, lam_ref,
                y_ref, ht_ref, xs_ref, h_ref, *, tc, nt):
    t = pl.program_id(2)
    halo = LRU_CONV - 1

    @pl.when(t == 0)
    def _():
        xs_ref[SUBLANES - halo:SUBLANES, :] = buf0_ref[...]
        h_ref[...] = h0_ref[...]

    x = u0_ref[...]
    xs_ref[SUBLANES:SUBLANES + tc, :] = x
    cw = cw_ref[...]
    u = cw[halo:halo + 1] * x + cb_ref[...]
    for s in range(1, LRU_CONV):
        u = u + cw[halo - s:halo - s + 1] * xs_ref[SUBLANES - s:SUBLANES - s + tc, :]
    xs_ref[0:SUBLANES, :] = xs_ref[tc:tc + SUBLANES, :]

    ub = u.astype(BF16)
    r = jax.nn.sigmoid(jnp.dot(ub, wa_ref[...], preferred_element_type=F32) + ba_ref[...])
    i = jax.nn.sigmoid(jnp.dot(ub, wx_ref[...], preferred_element_type=F32) + bx_ref[...])
    lam = lam_ref[...]
    softplus_neg_lam = jnp.maximum(-lam, 0.0) + jnp.log1p(jnp.exp(-jnp.abs(lam)))
    log_a = -LRU_C * r * softplus_neg_lam
    a = jnp.exp(log_a)
    th = jnp.tanh(log_a)
    b = jnp.sqrt(-2.0 * th / (1.0 - th)) * i * u

    row = lax.broadcasted_iota(jnp.int32, (tc, LRU_BLOCK), 0)
    s = 1
    while s < tc:
        a_sh = jnp.where(row >= s, pltpu.roll(a, s, axis=0), 1.0)
        b_sh = jnp.where(row >= s, pltpu.roll(b, s, axis=0), 0.0)
        b = a * b_sh + b
        a = a * a_sh
        s *= 2
    hs = a * h_ref[...] + b
    h_ref[...] = hs[tc - 1:tc, :]
    y_ref[...] = (hs * gate_ref[...]).astype(y_ref.dtype)

    @pl.when(t == nt - 1)
    def _():
        ht_ref[...] = hs[tc - 1:tc, :]


def _lru_scan(u0, gate, buf0, h0, cw, cb, wa, ba, wx, bx, lam, tc):
    bsz, t, r = u0.shape
    tc = min(tc, t)
    nt = t // tc
    nb = r // LRU_BLOCK
    vec = pl.BlockSpec((1, LRU_BLOCK), lambda b, n, k: (0, n))
    return pl.pallas_call(
        functools.partial(_lru_kernel, tc=tc, nt=nt),
        out_shape=[jax.ShapeDtypeStruct((bsz, t, r), BF16), jax.ShapeDtypeStruct((bsz, 1, r), F32)],
        grid=(bsz, nb, nt),
        in_specs=[
            pl.BlockSpec((None, tc, LRU_BLOCK), lambda b, n, k: (b, k, n)),
            pl.BlockSpec((None, tc, LRU_BLOCK), lambda b, n, k: (b, k, n)),
            pl.BlockSpec((None, LRU_CONV - 1, LRU_BLOCK), lambda b, n, k: (b, 0, n)),
            pl.BlockSpec((None, 1, LRU_BLOCK), lambda b, n, k: (b, 0, n)),
            pl.BlockSpec((LRU_CONV, LRU_BLOCK), lambda b, n, k: (0, n)),
            vec,
            pl.BlockSpec((None, LRU_BLOCK, LRU_BLOCK), lambda b, n, k: (n, 0, 0)),
            vec,
            pl.BlockSpec((None, LRU_BLOCK, LRU_BLOCK), lambda b, n, k: (n, 0, 0)),
            vec,
            vec,
        ],
        out_specs=[
            pl.BlockSpec((None, tc, LRU_BLOCK), lambda b, n, k: (b, k, n)),
            pl.BlockSpec((None, 1, LRU_BLOCK), lambda b, n, k: (b, 0, n)),
        ],
        scratch_shapes=[pltpu.VMEM((tc + SUBLANES, LRU_BLOCK), F32), pltpu.VMEM((1, LRU_BLOCK), F32)],
        compiler_params=_params(("parallel", "parallel", "arbitrary")),
        name="lru_scan",
    )(u0, gate, buf0, h0, cw, cb, wa, ba, wx, bx, lam)


def _lru_short_kernel(u0_ref, gate_ref, past_ref, h0_ref, cw_ref, cb_ref, wa_ref, ba_ref, wx_ref, bx_ref, lam_ref,
                      y_ref, ht_ref):
    x = u0_ref[...]
    bsz, t, _ = x.shape
    halo = LRU_CONV - 1
    pos = lax.broadcasted_iota(jnp.int32, x.shape, 1)
    past = past_ref[...]
    cw = cw_ref[...]
    u = cw[halo:halo + 1] * x + cb_ref[...]
    for s in range(1, LRU_CONV):
        shifted = jnp.where(pos >= s, pltpu.roll(x, s, axis=1), pltpu.roll(past, s, axis=1))
        u = u + cw[halo - s:halo - s + 1] * shifted

    u2 = u.reshape(bsz * t, LRU_BLOCK)
    ub = u2.astype(BF16)
    r = jax.nn.sigmoid(jnp.dot(ub, wa_ref[...], preferred_element_type=F32) + ba_ref[...])
    i = jax.nn.sigmoid(jnp.dot(ub, wx_ref[...], preferred_element_type=F32) + bx_ref[...])
    lam = lam_ref[...]
    softplus_neg_lam = jnp.maximum(-lam, 0.0) + jnp.log1p(jnp.exp(-jnp.abs(lam)))
    log_a = -LRU_C * r * softplus_neg_lam
    th = jnp.tanh(log_a)
    a = jnp.exp(log_a).reshape(x.shape)
    b = (jnp.sqrt(-2.0 * th / (1.0 - th)) * i * u2).reshape(x.shape)

    s = 1
    while s < t:
        a_sh = jnp.where(pos >= s, pltpu.roll(a, s, axis=1), 1.0)
        b_sh = jnp.where(pos >= s, pltpu.roll(b, s, axis=1), 0.0)
        b = a * b_sh + b
        a = a * a_sh
        s *= 2
    hs = a * h0_ref[...] + b
    y_ref[...] = (hs * gate_ref[...]).astype(y_ref.dtype)
    ht_ref[...] = hs[:, t - 1:t, :]


def _lru_scan_short(u0, gate, buf0, h0, cw, cb, wa, ba, wx, bx, lam):
    bsz, t, r = u0.shape
    past = jnp.concatenate([jnp.zeros((bsz, t - (LRU_CONV - 1), r), F32), buf0], axis=1)
    seq = pl.BlockSpec((bsz, t, LRU_BLOCK), lambda n: (0, 0, n))
    vec = pl.BlockSpec((1, LRU_BLOCK), lambda n: (0, n))
    mat = pl.BlockSpec((None, LRU_BLOCK, LRU_BLOCK), lambda n: (n, 0, 0))
    state = pl.BlockSpec((bsz, 1, LRU_BLOCK), lambda n: (0, 0, n))
    return pl.pallas_call(
        _lru_short_kernel,
        out_shape=[jax.ShapeDtypeStruct((bsz, t, r), BF16), jax.ShapeDtypeStruct((bsz, 1, r), F32)],
        grid=(r // LRU_BLOCK,),
        in_specs=[seq, seq, seq, state, pl.BlockSpec((LRU_CONV, LRU_BLOCK), lambda n: (0, n)), vec,
                  mat, vec, mat, vec, vec],
        out_specs=[seq, state],
        compiler_params=_params(("parallel",)),
        name="lru_scan_short",
    )(u0, gate, past, h0, cw, cb, wa, ba, wx, bx, lam)


CONV_HALO = 32
CONV_LANES = 256


def _cconv_kernel(u_ref, buf0_ref, w_ref, b_ref, lng_ref, lnb_ref, o_ref, xs_ref, acc_ref, *, tc):
    t = pl.program_id(1)
    d = u_ref.shape[-1]
    pad = CONV_HALO - (CONV_WIDTH - 1)

    @pl.when(t == 0)
    def _():
        xs_ref[pad:CONV_HALO, :] = buf0_ref[...]

    xs_ref[CONV_HALO:CONV_HALO + tc, :] = u_ref[...]
    for c in range(d // CONV_LANES):
        cs = slice(c * CONV_LANES, (c + 1) * CONV_LANES)
        acc = b_ref[:, cs]
        for r in range(SUBLANES):
            span = tc + SUBLANES * ((CONV_WIDTH - 1 - r) // SUBLANES)
            z = xs_ref[pad + r:pad + r + span, cs]
            part = None
            for k in range(r, CONV_WIDTH, SUBLANES):
                term = w_ref[k:k + 1, cs] * z[k - r:k - r + tc]
                part = term if part is None else part + term
            acc = acc + part
        acc_ref[:, cs] = acc
    xs_ref[0:CONV_HALO, :] = xs_ref[tc:tc + CONV_HALO, :]

    v = acc_ref[...]
    mu = jnp.mean(v, axis=-1, keepdims=True)
    vc = v - mu
    var = jnp.mean(vc * vc, axis=-1, keepdims=True)
    y = vc * lax.rsqrt(var + 1e-5) * lng_ref[...] + lnb_ref[...]
    o_ref[...] = (y * jax.nn.sigmoid(y)).astype(o_ref.dtype)


def _cconv(u, buf0, w, b, lng, lnb, tc):
    bsz, t, d = u.shape
    tc = min(tc, t)
    vec = pl.BlockSpec((1, d), lambda bb, k: (0, 0))
    return pl.pallas_call(
        functools.partial(_cconv_kernel, tc=tc),
        out_shape=jax.ShapeDtypeStruct((bsz, t, d), BF16),
        grid=(bsz, t // tc),
        in_specs=[
            pl.BlockSpec((None, tc, d), lambda bb, k: (bb, k, 0)),
            pl.BlockSpec((None, CONV_WIDTH - 1, d), lambda bb, k: (bb, 0, 0)),
            pl.BlockSpec((CONV_WIDTH, d), lambda bb, k: (0, 0)),
            vec, vec, vec,
        ],
        out_specs=pl.BlockSpec((None, tc, d), lambda bb, k: (bb, k, 0)),
        scratch_shapes=[pltpu.VMEM((tc + CONV_HALO, d), F32), pltpu.VMEM((tc, d), F32)],
        compiler_params=_params(("parallel", "arbitrary")),
        name="conformer_conv",
    )(u, buf0, w, b, lng, lnb)


QK_LOG2_SCALE = HEAD_DIM ** -0.5 * math.log2(math.e)


def _diff_finish(o1, o2, lam, g):
    o = o1 - lam * o2
    return _rms(o, g, 1e-5) * (1.0 - LAMBDA_INIT)


def _flash_kernel(qi_ref, ki_ref, lq1_ref, lk1_ref, lq2_ref, lk2_ref, g_ref, q_ref, k_ref, v_ref, o_ref,
                  m_ref, l_ref, acc_ref, *, tq, sb):
    step = pl.program_id(2)
    qi = qi_ref[step]
    ki = ki_ref[step]

    @pl.when(ki == 0)
    def _():
        m_ref[...] = jnp.full(m_ref.shape, NEG_INF, F32)
        l_ref[...] = jnp.zeros(l_ref.shape, F32)
        acc_ref[...] = jnp.zeros(acc_ref.shape, F32)

    def update(rows, keys, masked):
        nr = rows.stop - rows.start
        nk = keys.stop - keys.start
        if masked:
            keep = (lax.broadcasted_iota(jnp.int32, (nr, nk), 0) >= lax.broadcasted_iota(jnp.int32, (nr, nk), 1))
        v = v_ref[keys, :]
        for c in range(2):
            cs = slice(c * HEAD_DIM, (c + 1) * HEAD_DIM)
            s = _nt_dot(q_ref[rows, cs], k_ref[keys, cs])
            if masked:
                s = jnp.where(keep, s, NEG_INF)
            m_prev = m_ref[c, rows]
            m_new = jnp.maximum(m_prev, jnp.max(s, axis=-1, keepdims=True))
            alpha = jnp.exp2(m_prev - m_new)
            ps = [jnp.exp2(s[:, j * LANES:(j + 1) * LANES] - m_new) for j in range(nk // LANES)]
            l_new = alpha * l_ref[c, rows]
            for p in ps:
                l_new = l_new + p
            l_ref[c, rows] = l_new
            m_ref[c, rows] = m_new
            p = jnp.concatenate(ps, axis=1).astype(BF16)
            acc_ref[c, rows] = jnp.concatenate([alpha, alpha], axis=1) * acc_ref[c, rows] + jnp.dot(
                p, v, preferred_element_type=F32)

    nsub = tq // sb
    sub = [slice(i * sb, (i + 1) * sb) for i in range(nsub)]

    @pl.when(ki < qi)
    def _():
        for keys in sub:
            update(slice(0, tq), keys, False)

    @pl.when(ki == qi)
    def _():
        for i in range(nsub):
            for j in range(i + 1):
                update(sub[i], sub[j], i == j)
        lam = _lambda_full(lq1_ref[...], lk1_ref[...], lq2_ref[...], lk2_ref[...])
        o1 = acc_ref[0] / jnp.sum(l_ref[0], axis=-1, keepdims=True)
        o2 = acc_ref[1] / jnp.sum(l_ref[1], axis=-1, keepdims=True)
        o_ref[...] = _diff_finish(o1, o2, lam, g_ref[...]).astype(o_ref.dtype)


def _flash_diff_attn(q, k, v, lams, g, tq, sb):
    bsz, s, _ = q.shape
    hd2 = 2 * HEAD_DIM
    nq = s // tq
    pairs = [(i, j) for i in range(nq) for j in range(i + 1)]
    qi_tab = jnp.asarray([p[0] for p in pairs], jnp.int32)
    ki_tab = jnp.asarray([p[1] for p in pairs], jnp.int32)
    small = pl.BlockSpec((1, HEAD_DIM), lambda b, h, t, qt, kt: (0, 0))
    q_spec = pl.BlockSpec((None, tq, hd2), lambda b, h, t, qt, kt: (b, qt[t], h))
    kv_spec = pl.BlockSpec((None, tq, hd2), lambda b, h, t, qt, kt: (b, kt[t], h))
    grid_spec = pltpu.PrefetchScalarGridSpec(
        num_scalar_prefetch=2,
        grid=(bsz, ATTN_HEADS, len(pairs)),
        in_specs=[small, small, small, small, pl.BlockSpec((1, hd2), lambda b, h, t, qt, kt: (0, 0)),
                  q_spec, kv_spec, kv_spec],
        out_specs=q_spec,
        scratch_shapes=[pltpu.VMEM((2, tq, LANES), F32), pltpu.VMEM((2, tq, LANES), F32),
                        pltpu.VMEM((2, tq, hd2), F32)],
    )
    return pl.pallas_call(
        functools.partial(_flash_kernel, tq=tq, sb=sb),
        out_shape=jax.ShapeDtypeStruct(q.shape, BF16),
        grid_spec=grid_spec,
        compiler_params=_params(("parallel", "parallel", "arbitrary")),
        name="flash_diff_attn",
    )(qi_tab, ki_tab, *lams, g, q, k, v)


def _paged_kernel(pt_ref, lq1_ref, lk1_ref, lq2_ref, lk2_ref, g_ref, q_ref, kn_ref, vn_ref, *rest, pp, nj):
    del pt_ref
    k_refs = rest[:pp]
    v_refs = rest[pp:2 * pp]
    o_ref = rest[2 * pp]
    wq_ref, bias_ref, m_ref, l_ref, acc_ref = rest[2 * pp + 1:]
    j = pl.program_id(1)
    t = q_ref.shape[0]
    rows = 2 * ATTN_HEADS * t
    hd2 = 2 * HEAD_DIM
    nkeys = PAGE_SIZE * ATTN_HEADS

    @pl.when(j == 0)
    def _():
        q = q_ref[...]
        lane = lax.broadcasted_iota(jnp.int32, (t, hd2), 1)
        pieces = []
        for h in range(ATTN_HEADS):
            qh = q[:, h * hd2:(h + 1) * hd2]
            pieces.append(jnp.where(lane < HEAD_DIM, qh, 0.0))
            pieces.append(jnp.where(lane >= HEAD_DIM, qh, 0.0))
        wq_ref[...] = jnp.concatenate(pieces, axis=0).astype(BF16)
        row_head = lax.broadcasted_iota(jnp.int32, (rows, nkeys), 0) // (2 * t)
        key_head = lax.broadcasted_iota(jnp.int32, (rows, nkeys), 1) % ATTN_HEADS
        bias_ref[...] = jnp.where(row_head == key_head, 0.0, NEG_INF)
        m_ref[...] = jnp.full(m_ref.shape, NEG_INF, F32)
        l_ref[...] = jnp.zeros(l_ref.shape, F32)
        acc_ref[...] = jnp.zeros(acc_ref.shape, F32)

    def update(k2s, v2s, biases):
        wq = wq_ref[...]
        ss = [_nt_dot(wq, k2.astype(BF16)) + bias for k2, bias in zip(k2s, biases)]
        m_prev = m_ref[...]
        m_new = m_prev
        for s in ss:
            m_new = jnp.maximum(m_new, jnp.max(s, axis=-1, keepdims=True))
        alpha = jnp.exp2(m_prev - m_new)
        ps = [jnp.exp2(s - m_new) for s in ss]
        l_new = alpha * l_ref[...]
        pv = None
        for p, v2 in zip(ps, v2s):
            l_new = l_new + jnp.sum(p, axis=-1, keepdims=True)
            d = jnp.dot(p.astype(BF16), v2.astype(BF16), preferred_element_type=F32)
            pv = d if pv is None else pv + d
        l_ref[...] = l_new
        m_ref[...] = m_new
        acc_ref[...] = alpha * acc_ref[...] + pv

    update([kr[...].reshape(nkeys, hd2) for kr in k_refs], [vr[...].reshape(nkeys, hd2) for vr in v_refs],
           [bias_ref[...]] * pp)

    @pl.when(j == nj - 1)
    def _():
        n_new = t * ATTN_HEADS
        row = lax.broadcasted_iota(jnp.int32, (rows, n_new), 0)
        col = lax.broadcasted_iota(jnp.int32, (rows, n_new), 1)
        keep = (row // (2 * t) == col % ATTN_HEADS) & (col // ATTN_HEADS <= row % t)
        update([kn_ref[...].reshape(n_new, hd2)], [vn_ref[...].reshape(n_new, hd2)],
               [jnp.where(keep, 0.0, NEG_INF)])
        lam = _lambda_full(lq1_ref[...], lk1_ref[...], lq2_ref[...], lk2_ref[...])
        o = acc_ref[...] / l_ref[...]
        for h in range(ATTN_HEADS):
            o1 = o[h * 2 * t:h * 2 * t + t]
            o2 = o[h * 2 * t + t:(h + 1) * 2 * t]
            o_ref[:, h * hd2:(h + 1) * hd2] = _diff_finish(o1, o2, lam, g_ref[...]).astype(o_ref.dtype)


def _paged_diff_attn(q, k_new, v_new, cache_k, cache_v, page_table, lams, g, pp):
    bsz, t, dk = q.shape
    hd2 = 2 * HEAD_DIM
    n_pages = page_table.shape[1]
    nj = n_pages // pp
    rows = 2 * ATTN_HEADS * t
    small = pl.BlockSpec((1, HEAD_DIM), lambda b, j, pt: (0, 0))
    tok = pl.BlockSpec((None, t, ATTN_HEADS, hd2), lambda b, j, pt: (b, 0, 0, 0))
    page_specs = [
        pl.BlockSpec((None, PAGE_SIZE, ATTN_HEADS, hd2),
                     functools.partial(lambda b, j, pt, r: (pt[b, j * pp + r], 0, 0, 0), r=r))
        for r in range(pp)
    ]
    grid_spec = pltpu.PrefetchScalarGridSpec(
        num_scalar_prefetch=1,
        grid=(bsz, nj),
        in_specs=[small, small, small, small, pl.BlockSpec((1, hd2), lambda b, j, pt: (0, 0)),
                  pl.BlockSpec((None, t, dk), lambda b, j, pt: (b, 0, 0)), tok, tok] + page_specs + page_specs,
        out_specs=pl.BlockSpec((None, t, dk), lambda b, j, pt: (b, 0, 0)),
        scratch_shapes=[pltpu.VMEM((rows, hd2), BF16), pltpu.VMEM((rows, PAGE_SIZE * ATTN_HEADS), F32),
                        pltpu.VMEM((rows, 1), F32), pltpu.VMEM((rows, 1), F32), pltpu.VMEM((rows, hd2), F32)],
    )
    return pl.pallas_call(
        functools.partial(_paged_kernel, pp=pp, nj=nj),
        out_shape=jax.ShapeDtypeStruct(q.shape, BF16),
        grid_spec=grid_spec,
        compiler_params=_params(("parallel", "arbitrary")),
        name="paged_diff_attn",
    )(page_table, *lams, g, q, k_new, v_new, *([cache_k] * pp), *([cache_v] * pp))


FFN_TM = 1024
FFN_TF = 512
PROJ_TM = 1024
PROJ_TN = 512
OUT_TM = 512
LRU_TC = 256
CCONV_TC = 128
FLASH_TQ = 2048
FLASH_SUB = 512
PAGES_PER_STEP = 8


def _gelu_gate(accs):
    return [jax.nn.gelu(accs[0], approximate=True), accs[1]]


def _glu(accs):
    return [accs[0] * jax.nn.sigmoid(accs[1])]


def _qkv(accs):
    q, k, v = accs
    return [q * QK_LOG2_SCALE, k, v, k, v]


def _row(v):
    return v.reshape(1, -1)


def _trunk(x, bsz, t, lru_h0, lru_buf0, conv_buf0, past, w, ffn_bf16):
    d = x.shape[-1]
    new_h, new_buf = [], []
    new_conv = new_k = new_v = None
    li = 0

    def ffn(x, layer, half):
        gpre, gpost = _row(w["norm_pre"][layer, 2 * half]), _row(w["norm_post"][layer, 2 * half])
        if (layer, half) in ffn_bf16:
            return _ffn(x, gpre, *ffn_bf16[layer, half], gpost, FFN_TM)
        x, *ffn_bf16[layer, half] = _ffn_cast(x, gpre, w["ffn_w_gate"], w["ffn_w_up"], w["ffn_w_down"], gpost,
                                              FFN_TF, (layer, half))
        return x

    for layer in range(4):
        x = ffn(x, layer, 0)
        gpre = _row(w["norm_pre"][layer, 1])
        gpost = _row(w["norm_post"][layer, 1])
        kind = layer % 3
        if kind == 0:
            r = w["lru_w_gate"].shape[-1]
            gate, u0 = _norm_proj(x, gpre, [(w["lru_w_gate"][li], 0), (w["lru_w_in"][li], 0)], [], _gelu_gate,
                                  [F32, F32], r, PROJ_TM, PROJ_TN, "lru_in")
            u0 = u0.reshape(bsz, t, r)
            scan_args = (u0, gate.reshape(bsz, t, r), lru_buf0[li], lru_h0[li].reshape(bsz, 1, r),
                         w["lru_conv_w"][li], _row(w["lru_conv_b"][li]), w["lru_w_a"][li], _row(w["lru_b_a"][li]),
                         w["lru_w_x"][li], _row(w["lru_b_x"][li]), _row(w["lru_lambda"][li]))
            y, ht = _lru_scan_short(*scan_args) if t == SUBLANES else _lru_scan(*scan_args, LRU_TC)
            new_h.append(ht.reshape(bsz, r))
            new_buf.append(jnp.concatenate([lru_buf0[li], u0], axis=1)[:, t:])
            x = _proj_res(y.reshape(bsz * t, r), w["lru_w_out"][li], None, x, gpost, OUT_TM, "lru_out")
            li += 1
        elif kind == 1:
            b1 = _row(w["conv_b_pw1"])
            (u,) = _norm_proj(x, gpre, [(w["conv_w_pw1"], 0), (w["conv_w_pw1"], d)], [(b1, 0), (b1, d)], _glu,
                              [F32], d, PROJ_TM, PROJ_TN, "conformer_pw1")
            u = u.reshape(bsz, t, d)
            v = _cconv(u, conv_buf0, w["conv_w_dw"], _row(w["conv_b_dw"]), _row(w["conv_ln_g"]),
                       _row(w["conv_ln_b"]), CCONV_TC)
            new_conv = jnp.concatenate([conv_buf0, u], axis=1)[:, t:]
            x = _proj_res(v.reshape(bsz * t, d), w["conv_w_pw2"], _row(w["conv_b_pw2"]), x, gpost, OUT_TM,
                          "conformer_pw2")
        else:
            q_dtype = BF16 if past is None else F32
            qs, k, v, kb, vb = _norm_proj(
                x, gpre, [(w["attn_w_q"], 0), (w["attn_w_k"], 0), (w["attn_w_v"], 0)], [], _qkv,
                [q_dtype, F32, F32, BF16, BF16], d, PROJ_TM, PROJ_TN, "attn_qkv")
            lams = [_row(w[n]) for n in ("attn_lambda_q1", "attn_lambda_k1", "attn_lambda_q2", "attn_lambda_k2")]
            g = _row(w["attn_subln_g"])
            new_k = k.reshape(bsz, t, ATTN_HEADS, 2 * HEAD_DIM)
            new_v = v.reshape(bsz, t, ATTN_HEADS, 2 * HEAD_DIM)
            if past is None:
                o = _flash_diff_attn(qs.reshape(bsz, t, d), kb.reshape(bsz, t, d), vb.reshape(bsz, t, d),
                                     lams, g, FLASH_TQ, FLASH_SUB)
            else:
                cache_k, cache_v, page_table = past
                o = _paged_diff_attn(qs.reshape(bsz, t, d), new_k, new_v, cache_k, cache_v, page_table, lams, g,
                                     PAGES_PER_STEP)
            x = _proj_res(o.reshape(bsz * t, d), w["attn_w_o"], None, x, gpost, OUT_TM, "attn_out")
        x = ffn(x, layer, 1)
    return x.reshape(bsz, t, d), new_k, new_v, jnp.stack(new_h), jnp.stack(new_buf), new_conv


def kernel(x_prompt, x_sample, cache_k, cache_v, page_table, state_lru_h, state_lru_conv, state_conv, norm_pre, norm_post, ffn_w_gate, ffn_w_up, ffn_w_down, lru_w_gate, lru_w_in, lru_conv_w, lru_conv_b, lru_w_a, lru_b_a, lru_w_x, lru_b_x, lru_lambda, lru_w_out, conv_w_pw1, conv_b_pw1, conv_w_dw, conv_b_dw, conv_ln_g, conv_ln_b, conv_w_pw2, conv_b_pw2, attn_w_q, attn_w_k, attn_w_v, attn_w_o, attn_lambda_q1, attn_lambda_k1, attn_lambda_q2, attn_lambda_k2, attn_subln_g):
    w = dict(
        norm_pre=norm_pre, norm_post=norm_post,
        ffn_w_gate=ffn_w_gate, ffn_w_up=ffn_w_up, ffn_w_down=ffn_w_down,
        lru_w_gate=lru_w_gate.astype(BF16), lru_w_in=lru_w_in.astype(BF16),
        lru_conv_w=lru_conv_w, lru_conv_b=lru_conv_b,
        lru_w_a=lru_w_a.astype(BF16), lru_b_a=lru_b_a, lru_w_x=lru_w_x.astype(BF16), lru_b_x=lru_b_x,
        lru_lambda=lru_lambda, lru_w_out=lru_w_out.astype(BF16),
        conv_w_pw1=conv_w_pw1.astype(BF16), conv_b_pw1=conv_b_pw1, conv_w_dw=conv_w_dw, conv_b_dw=conv_b_dw,
        conv_ln_g=conv_ln_g, conv_ln_b=conv_ln_b, conv_w_pw2=conv_w_pw2.astype(BF16), conv_b_pw2=conv_b_pw2,
        attn_w_q=attn_w_q.astype(BF16), attn_w_k=attn_w_k.astype(BF16), attn_w_v=attn_w_v.astype(BF16),
        attn_w_o=attn_w_o.astype(BF16),
        attn_lambda_q1=attn_lambda_q1, attn_lambda_k1=attn_lambda_k1,
        attn_lambda_q2=attn_lambda_q2, attn_lambda_k2=attn_lambda_k2, attn_subln_g=attn_subln_g,
    )
    bp, s, d = x_prompt.shape
    db, ds, _ = x_sample.shape
    n_lru = state_lru_h.shape[0]
    r = state_lru_h.shape[-1]

    ffn_bf16 = {}
    y_s, k_s, v_s, h_s, lbuf_s, cbuf_s = _trunk(
        x_sample.reshape(db * ds, d), db, ds, state_lru_h, state_lru_conv, state_conv,
        (cache_k, cache_v, page_table), w, ffn_bf16)

    y_p, k_p, v_p, h_p, lbuf_p, cbuf_p = _trunk(
        x_prompt.reshape(bp * s, d), bp, s,
        jnp.zeros((n_lru, bp, r), F32), jnp.zeros((n_lru, bp, LRU_CONV - 1, r), F32),
        jnp.zeros((bp, CONV_WIDTH - 1, d), F32), None, w, ffn_bf16)

    return (y_p, y_s, k_p, v_p, h_p, lbuf_p, cbuf_p, k_s, v_s, h_s, lbuf_s, cbuf_s)
```
